```python
import math
import jax, jax.numpy as jnp
from jax import lax
import numpy as np


D_MODEL = 2048
BATCH = 1
SEQ = 16384
DEPTH = 2
DEC_BATCH = 4
DEC_SEQ = 4096
PAST_LEN = 128

GRID_W = 64
Q_BLOCK = 128
HEAD_DIM = 128
ROPE_THETA = 500000.0
DIFF_HEADS = 8
DIFF_QK_DIM = HEAD_DIM // 2
DIFF_V_DIM = HEAD_DIM
DIFF_ROT = DIFF_QK_DIM // 4
NA_HEADS = 8
NA_DIM = HEAD_DIM
NA_KH = 8
NA_KW = 16
MLA_HEADS = 8
MLA_Q_RANK = 512
MLA_KV_RANK = 256
MLA_NOPE = 128
MLA_ROPE = 64
MLA_V = 128
MLA_THETA = 10000.0
GQA_HEADS = 8
GQA_KV_HEADS = 2
GQA_DIM = HEAD_DIM
AXIAL_THETA = 10000.0
N_GROUPS = 4
EXPERTS_PER_GROUP = 8
N_EXPERTS = N_GROUPS * EXPERTS_PER_GROUP
TOP_K = 2
D_EXPERT = 512
MOE_BLOCK = 128
DN_ALPHA = (2 * DEPTH) ** 0.25
DN_BETA = (8 * DEPTH) ** -0.25
LN_EPS = 1e-5
RMS_EPS = 1e-6
N_EVEN = (DEPTH + 1) // 2
N_ODD = DEPTH // 2
EVEN_IN = 3 * DIFF_HEADS * HEAD_DIM + 3 * NA_HEADS * NA_DIM
EVEN_OUT = DIFF_HEADS * DIFF_V_DIM + NA_HEADS * NA_DIM
ODD_IN = MLA_Q_RANK + MLA_KV_RANK + MLA_ROPE + (GQA_HEADS + 2 * GQA_KV_HEADS) * GQA_DIM
ODD_OUT = MLA_HEADS * MLA_V + GQA_HEADS * GQA_DIM

kernel_name = "hybrid_diffattn_natten_mla_gqa_hmoe_encoder"


def _rms_norm(x, g, eps=RMS_EPS):
    xf = x.astype(jnp.float32)
    y = xf * lax.rsqrt(jnp.mean(xf * xf, axis=-1, keepdims=True) + eps)
    return (y * g.astype(jnp.float32)).astype(x.dtype)


def _layer_norm(x, g, b, eps=LN_EPS):
    xf = x.astype(jnp.float32)
    mu = jnp.mean(xf, axis=-1, keepdims=True)
    xc = xf - mu
    var = jnp.mean(xc * xc, axis=-1, keepdims=True)
    y = xc * lax.rsqrt(var + eps) * g.astype(jnp.float32) + b.astype(jnp.float32)
    return y.astype(x.dtype)


def _rope_cos_sin(pos, dim, theta):
    inv = theta ** (-jnp.arange(0, dim, 2, dtype=jnp.float32) / dim)
    ang = pos.astype(jnp.float32)[:, None] * inv[None, :]
    return jnp.cos(ang), jnp.sin(ang)


def _rotate(x, cos, sin):
    shape = (cos.shape[0],) + (1,) * (x.ndim - 3) + (cos.shape[1],)
    c = cos.reshape(shape)
    s = sin.reshape(shape)
    xf = x.astype(jnp.float32)
    x1, x2 = jnp.split(xf, 2, axis=-1)
    return jnp.concatenate([x1 * c - x2 * s, x2 * c + x1 * s], axis=-1).astype(x.dtype)


def _partial_rope(x, cos, sin, rot):
    return jnp.concatenate([_rotate(x[..., :rot], cos, sin), x[..., rot:]], axis=-1)


def _axial_rope(x, cos_r, sin_r, cos_c, sin_c):
    half = x.shape[-1] // 2
    return jnp.concatenate([_rotate(x[..., :half], cos_r, sin_r),
                            _rotate(x[..., half:], cos_c, sin_c)], axis=-1)


def _sweep_queries(fn, *qs):
    B, S = qs[0].shape[:2]
    nb = S // Q_BLOCK
    blocks = tuple(jnp.moveaxis(q.reshape((B, nb, Q_BLOCK) + q.shape[2:]), 1, 0) for q in qs)
    out = lax.map(lambda a: fn(*a), blocks)
    out = jnp.moveaxis(out, 0, 1)
    return out.reshape((B, S) + out.shape[3:])


def _diff_attention(q, k, v, lam, subln_g, lambda_init):
    scale = DIFF_QK_DIM ** -0.5

    def blk(qb):
        s = jnp.einsum('bqhmd,bkhmd->bhmqk', qb, k).astype(jnp.float32) * scale
        p = jax.nn.softmax(s, axis=-1)
        a = p[:, :, 0] - lam * p[:, :, 1]
        return jnp.einsum('bhqk,bkhe->bqhe', a.astype(v.dtype), v)

    o = _sweep_queries(blk, q)
    return _rms_norm(o, subln_g, 1e-5) * (1.0 - lambda_init)


def _neighbourhood_attention(q, k, v, rpb, rows):
    B, S, H, d = q.shape
    kh = min(NA_KH, rows)
    qg = q.reshape(B, rows, GRID_W, H, d)
    kg = k.reshape(B, rows, GRID_W, H, d)
    vg = v.reshape(B, rows, GRID_W, H, d)
    cols = np.arange(GRID_W)
    col_start = np.clip(cols - NA_KW // 2, 0, GRID_W - NA_KW)
    col_idx = col_start[:, None] + np.arange(NA_KW)[None, :]
    dc = jnp.asarray(col_idx - cols[:, None] + (NA_KW - 1))
    scale = d ** -0.5

    def row(r):
        rs = jnp.clip(r - kh // 2, 0, rows - kh)
        kr = lax.dynamic_slice_in_dim(kg, rs, kh, axis=1)
        vr = lax.dynamic_slice_in_dim(vg, rs, kh, axis=1)
        kn = jnp.take(kr, col_idx, axis=2)
        vn = jnp.take(vr, col_idx, axis=2)
        qr = lax.dynamic_index_in_dim(qg, r, axis=1, keepdims=False)
        dr = rs + jnp.arange(kh) - r + (NA_KH - 1)
        bias = rpb[:, dr[None, :, None], dc[:, None, :]]
        s = jnp.einsum('bwhd,biwjhd->bhwij', qr, kn).astype(jnp.float32) * scale
        s = s + bias.astype(jnp.float32)[None]
        p = jax.nn.softmax(s.reshape(B, H, GRID_W, kh * NA_KW), axis=-1).reshape(s.shape)
        return jnp.einsum('bhwij,biwjhd->bwhd', p.astype(v.dtype), vn)

    o = lax.map(row, jnp.arange(rows))
    return jnp.moveaxis(o, 0, 1).reshape(B, S, H, d)


def _mla(q_c, kv_c, k_pe, q_norm, w_q_up, kv_norm, w_kv_up, cos, sin):
    B, S, _ = q_c.shape
    q = (_rms_norm(q_c, q_norm) @ w_q_up).reshape(B, S, MLA_HEADS, MLA_NOPE + MLA_ROPE)
    q_nope = q[..., :MLA_NOPE]
    q_pe = _rotate(q[..., MLA_NOPE:], cos, sin)
    kv = (_rms_norm(kv_c, kv_norm) @ w_kv_up).reshape(B, S, MLA_HEADS, MLA_NOPE + MLA_V)
    k_nope = kv[..., :MLA_NOPE]
    v = kv[..., MLA_NOPE:]
    k_pe = _rotate(k_pe, cos, sin)
    scale = (MLA_NOPE + MLA_ROPE) ** -0.5

    def blk(qn, qp):
        s = (jnp.einsum('bqhd,bkhd->bhqk', qn, k_nope)
             + jnp.einsum('bqhr,bkr->bhqk', qp, k_pe)).astype(jnp.float32) * scale
        p = jax.nn.softmax(s, axis=-1)
        return jnp.einsum('bhqk,bkhd->bqhd', p.astype(v.dtype), v)

    o = _sweep_queries(blk, q_nope, q_pe)
    return o.reshape(B, S, MLA_HEADS * MLA_V)


def _gqa_axial(q, k, v, q_norm, k_norm, cos_r, sin_r, cos_c, sin_c):
    B, S = q.shape[:2]
    q = _axial_rope(_rms_norm(q, q_norm), cos_r, sin_r, cos_c, sin_c)
    k = _axial_rope(_rms_norm(k, k_norm), cos_r, sin_r, cos_c, sin_c)
    group = GQA_HEADS // GQA_KV_HEADS
    q = q.reshape(B, S, GQA_KV_HEADS, group, GQA_DIM)
    scale = GQA_DIM ** -0.5

    def blk(qb):
        s = jnp.einsum('bqngd,bknd->bngqk', qb, k).astype(jnp.float32) * scale
        p = jax.nn.softmax(s, axis=-1)
        return jnp.einsum('bngqk,bknd->bqngd', p.astype(v.dtype), v)

    o = _sweep_queries(blk, q)
    return o.reshape(B, S, GQA_HEADS * GQA_DIM)


def _even_mixer(x, w_in, w_out, lam_vec, subln_g, rpb, layer_idx):
    B, S, _ = x.shape
    rows = S // GRID_W
    pos = jnp.arange(S)
    h = x @ w_in
    dq = DIFF_HEADS * HEAD_DIM
    nq = NA_HEADS * NA_DIM
    a_q, a_k, a_v, b_q, b_k, b_v = jnp.split(
        h, [dq, 2 * dq, 3 * dq, 3 * dq + nq, 3 * dq + 2 * nq], axis=-1)
    cos, sin = _rope_cos_sin(pos, DIFF_ROT, ROPE_THETA)
    a_q = _partial_rope(a_q.reshape(B, S, DIFF_HEADS, 2, DIFF_QK_DIM), cos, sin, DIFF_ROT)
    a_k = _partial_rope(a_k.reshape(B, S, DIFF_HEADS, 2, DIFF_QK_DIM), cos, sin, DIFF_ROT)
    a_v = a_v.reshape(B, S, DIFF_HEADS, DIFF_V_DIM)
    lambda_init = 0.8 - 0.6 * math.exp(-0.3 * layer_idx)
    lv = lam_vec.astype(jnp.float32)
    lam = jnp.exp(jnp.sum(lv[0] * lv[1])) - jnp.exp(jnp.sum(lv[2] * lv[3])) + lambda_init
    o_a = _diff_attention(a_q, a_k, a_v, lam, subln_g, lambda_init)
    o_b = _neighbourhood_attention(b_q.reshape(B, S, NA_HEADS, NA_DIM),
                                   b_k.reshape(B, S, NA_HEADS, NA_DIM),
                                   b_v.reshape(B, S, NA_HEADS, NA_DIM), rpb, rows)
    o = jnp.concatenate([o_a.reshape(B, S, -1), o_b.reshape(B, S, -1)], axis=-1)
    return o @ w_out


def _odd_mixer(x, w_in, w_out, q_norm, w_q_up, kv_norm, w_kv_up, g_q_norm, g_k_norm):
    B, S, _ = x.shape
    pos = jnp.arange(S)
    h = x @ w_in
    s1 = MLA_Q_RANK
    s2 = s1 + MLA_KV_RANK
    s3 = s2 + MLA_ROPE
    s4 = s3 + GQA_HEADS * GQA_DIM
    s5 = s4 + GQA_KV_HEADS * GQA_DIM
    q_c, kv_c, k_pe, g_q, g_k, g_v = jnp.split(h, [s1, s2, s3, s4, s5], axis=-1)
    cos, sin = _rope_cos_sin(pos, MLA_ROPE, MLA_THETA)
    o_c = _mla(q_c, kv_c, k_pe, q_norm, w_q_up, kv_norm, w_kv_up, cos, sin)
    cos_r, sin_r = _rope_cos_sin(pos // GRID_W, GQA_DIM // 2, AXIAL_THETA)
    cos_c, sin_c = _rope_cos_sin(pos % GRID_W, GQA_DIM // 2, AXIAL_THETA)
    o_d = _gqa_axial(g_q.reshape(B, S, GQA_HEADS, GQA_DIM),
                     g_k.reshape(B, S, GQA_KV_HEADS, GQA_DIM),
                     g_v.reshape(B, S, GQA_KV_HEADS, GQA_DIM),
                     g_q_norm, g_k_norm, cos_r, sin_r, cos_c, sin_c)
    o = jnp.concatenate([o_c, o_d], axis=-1)
    return o @ w_out


def _grouped_experts(xt, eid, gate, w1, w3, w2):
    T, D = xt.shape
    A = T * TOP_K
    flat_e = eid.reshape(A)
    order = jnp.argsort(flat_e)
    sorted_e = flat_e[order]
    counts = jnp.zeros((N_EXPERTS,), jnp.int32).at[flat_e].add(1)
    padded = (counts + MOE_BLOCK - 1) // MOE_BLOCK * MOE_BLOCK
    pend = jnp.cumsum(padded)
    pstart = pend - padded
    start = jnp.cumsum(counts) - counts
    rank = jnp.arange(A, dtype=jnp.int32) - start[sorted_e]
    dest = pstart[sorted_e] + rank
    nb = -(-A // MOE_BLOCK) + N_EXPERTS
    P = nb * MOE_BLOCK
    buf_tok = jnp.full((P,), T, jnp.int32).at[dest].set((order // TOP_K).astype(jnp.int32))
    block_e = jnp.minimum(jnp.searchsorted(pend, jnp.arange(nb) * MOE_BLOCK, side='right'),
                          N_EXPERTS - 1)
    x_pad = jnp.concatenate([xt, jnp.zeros((1, D), xt.dtype)], axis=0)
    xb = x_pad[buf_tok].reshape(nb, MOE_BLOCK, D)

    def run(args):
        xi, e = args
        hmid = jax.nn.silu(xi @ w1[e]) * (xi @ w3[e])
        return hmid @ w2[e]

    yb = lax.map(run, (xb, block_e)).reshape(P, D)
    y_sorted = yb[dest]
    g_sorted = gate.reshape(A)[order].astype(xt.dtype)
    return jnp.zeros((T, D), xt.dtype).at[order // TOP_K].add(y_sorted * g_sorted[:, None])


def _hier_moe(x, w_group, b_group, w_expert, b_expert, w1, w3, w2):
    B, S, D = x.shape
    T = B * S
    xt = x.reshape(T, D)
    g_logits = (xt @ w_group).astype(jnp.float32) + b_group.astype(jnp.float32)
    g_prob = jax.nn.softmax(g_logits, axis=-1)
    g_sel = jnp.argmax(g_logits, axis=-1)
    e_logits = ((xt @ w_expert).astype(jnp.float32) + b_expert.astype(jnp.float32))
    e_logits = e_logits.reshape(T, N_GROUPS, EXPERTS_PER_GROUP)
    e_in = jnp.take_along_axis(e_logits, g_sel[:, None, None], axis=1)[:, 0]
    top_v, top_i = lax.top_k(e_in, TOP_K)
    gate = jax.nn.softmax(top_v, axis=-1) * jnp.take_along_axis(g_prob, g_sel[:, None], axis=1)
    eid = g_sel[:, None] * EXPERTS_PER_GROUP + top_i
    y = _grouped_experts(xt, eid, gate, w1, w3, w2)
    return y.reshape(B, S, D)


def _trunk(x, ev_w_in, ev_w_out, diff_lambda, diff_subln, na_rpb,
           od_w_in, od_w_out, mla_q_norm, mla_w_q_up, mla_kv_norm, mla_w_kv_up,
           gqa_q_norm, gqa_k_norm, ln1_g, ln1_b, ln2_g, ln2_b,
           moe_w_group, moe_b_group, moe_w_expert, moe_b_expert, moe_w1, moe_w3, moe_w2):
    for l in range(DEPTH):
        i = l // 2
        if l % 2 == 0:
            mix = _even_mixer(x, ev_w_in[i], ev_w_out[i], diff_lambda[i], diff_subln[i],
                              na_rpb[i], l)
        else:
            mix = _odd_mixer(x, od_w_in[i], od_w_out[i], mla_q_norm[i], mla_w_q_up[i],
                             mla_kv_norm[i], mla_w_kv_up[i], gqa_q_norm[i], gqa_k_norm[i])
        x = _layer_norm(DN_ALPHA * x + mix, ln1_g[l], ln1_b[l])
        ff = _hier_moe(x, moe_w_group[l], moe_b_group[l], moe_w_expert[l], moe_b_expert[l],
                       moe_w1[l], moe_w3[l], moe_w2[l])
        x = _layer_norm(DN_ALPHA * x + ff, ln2_g[l], ln2_b[l])
    return x


def _normal(k, shape, scale):
    return jax.random.normal(k, shape, jnp.float32) * scale


def setup_inputs(seed: int = 0) -> dict:
    key = jax.random.key(seed)
    ks = jax.random.split(key, 26)
    D = D_MODEL
    return {
        "x_prompt": _normal(ks[0], (BATCH, SEQ, D), 1.0),
        "x_sample": _normal(ks[1], (DEC_BATCH, DEC_SEQ, D), 1.0),
        "ev_w_in": _normal(ks[2], (N_EVEN, D, EVEN_IN), D ** -0.5),
        "ev_w_out": _normal(ks[3], (N_EVEN, EVEN_OUT, D), DN_BETA * EVEN_OUT ** -0.5),
        "diff_lambda": _normal(ks[4], (N_EVEN, 4, DIFF_QK_DIM), 0.1),
        "diff_subln": 1.0 + _normal(ks[5], (N_EVEN, DIFF_V_DIM), 0.02),
        "na_rpb": _normal(ks[6], (N_EVEN, NA_HEADS, 2 * NA_KH - 1, 2 * NA_KW - 1), 0.02),
        "od_w_in": _normal(ks[7], (N_ODD, D, ODD_IN), D ** -0.5),
        "od_w_out": _normal(ks[8], (N_ODD, ODD_OUT, D), DN_BETA * ODD_OUT ** -0.5),
        "mla_q_norm": 1.0 + _normal(ks[9], (N_ODD, MLA_Q_RANK), 0.02),
        "mla_w_q_up": _normal(ks[10], (N_ODD, MLA_Q_RANK, MLA_HEADS * (MLA_NOPE + MLA_ROPE)),
                              MLA_Q_RANK ** -0.5),
        "mla_kv_norm": 1.0 + _normal(ks[11], (N_ODD, MLA_KV_RANK), 0.02),
        "mla_w_kv_up": _normal(ks[12], (N_ODD, MLA_KV_RANK, MLA_HEADS * (MLA_NOPE + MLA_V)),
                               MLA_KV_RANK ** -0.5),
        "gqa_q_norm": 1.0 + _normal(ks[13], (N_ODD, GQA_DIM), 0.02),
        "gqa_k_norm": 1.0 + _normal(ks[14], (N_ODD, GQA_DIM), 0.02),
        "ln1_g": 1.0 + _normal(ks[15], (DEPTH, D), 0.02),
        "ln1_b": _normal(ks[16], (DEPTH, D), 0.02),
        "ln2_g": 1.0 + _normal(ks[17], (DEPTH, D), 0.02),
        "ln2_b": _normal(ks[18], (DEPTH, D), 0.02),
        "moe_w_group": _normal(ks[19], (DEPTH, D, N_GROUPS), D ** -0.5),
        "moe_b_group": _normal(ks[20], (DEPTH, N_GROUPS), 0.01),
        "moe_w_expert": _normal(ks[21], (DEPTH, D, N_EXPERTS), D ** -0.5),
        "moe_b_expert": _normal(ks[22], (DEPTH, N_EXPERTS), 0.01),
        "moe_w1": _normal(ks[23], (DEPTH, N_EXPERTS, D, D_EXPERT), D ** -0.5),
        "moe_w3": _normal(ks[24], (DEPTH, N_EXPERTS, D, D_EXPERT), D ** -0.5),
        "moe_w2": _normal(ks[25], (DEPTH, N_EXPERTS, D_EXPERT, D), DN_BETA * D_EXPERT ** -0.5),
    }


def reference(x_prompt, x_sample, ev_w_in, ev_w_out, diff_lambda, diff_subln, na_rpb,
              od_w_in, od_w_out, mla_q_norm, mla_w_q_up, mla_kv_norm, mla_w_kv_up,
              gqa_q_norm, gqa_k_norm, ln1_g, ln1_b, ln2_g, ln2_b,
              moe_w_group, moe_b_group, moe_w_expert, moe_b_expert, moe_w1, moe_w3, moe_w2):
    y_prompt = _trunk(x_prompt, ev_w_in, ev_w_out, diff_lambda, diff_subln, na_rpb,
                      od_w_in, od_w_out, mla_q_norm, mla_w_q_up, mla_kv_norm, mla_w_kv_up,
                      gqa_q_norm, gqa_k_norm, ln1_g, ln1_b, ln2_g, ln2_b,
                      moe_w_group, moe_b_group, moe_w_expert, moe_b_expert, moe_w1, moe_w3, moe_w2)
    y_sample = _trunk(x_sample, ev_w_in, ev_w_out, diff_lambda, diff_subln, na_rpb,
                      od_w_in, od_w_out, mla_q_norm, mla_w_q_up, mla_kv_norm, mla_w_kv_up,
                      gqa_q_norm, gqa_k_norm, ln1_g, ln1_b, ln2_g, ln2_b,
                      moe_w_group, moe_b_group, moe_w_expert, moe_b_expert, moe_w1, moe_w3, moe_w2)
    return (y_prompt, y_sample)
```

```python
import functools
import math

import jax
import jax.numpy as jnp
from jax import lax
from jax.experimental import pallas as pl
from jax.experimental.pallas import tpu as pltpu

F32 = jnp.float32
BF16 = jnp.bfloat16

GRID_W = 64
HEAD_DIM = 128
ROPE_THETA = 500000.0
DIFF_HEADS = 8
DIFF_QK_DIM = HEAD_DIM // 2
DIFF_ROT = DIFF_QK_DIM // 4
NA_HEADS = 8
NA_KH = 8
NA_KW = 16
MLA_HEADS = 8
MLA_Q_RANK = 512
MLA_KV_RANK = 256
MLA_NOPE = 128
MLA_ROPE = 64
MLA_V = 128
MLA_THETA = 10000.0
GQA_HEADS = 8
GQA_KV_HEADS = 2
AXIAL_THETA = 10000.0
N_GROUPS = 4
EXPERTS_PER_GROUP = 8
N_EXPERTS = N_GROUPS * EXPERTS_PER_GROUP
TOP_K = 2
LN_EPS = 1e-5
RMS_EPS = 1e-6
SUBLN_EPS = 1e-5

LANES = 128
V7X_VMEM_BYTES = 64 * 1024 * 1024
VMEM_LIMIT = V7X_VMEM_BYTES * 7 // 8

LOG2E = math.log2(math.e)
NEG_BIG = -1e30

PROJ_TM = 512
PROJ_TN = 1024
ROW_TM = 256
FLASH_ROWS = 512
FLASH_TK = 512
NA_ROWS = 8
MOE_TB = 256


def _cparams(sem):
    return pltpu.CompilerParams(dimension_semantics=sem, vmem_limit_bytes=VMEM_LIMIT)


def _rope_tables(pos, lane_freq, lane_kind):
    if pos.ndim == 1:
        pos = pos[:, None]
    ang = pos * lane_freq[None, :]
    cos, sin = jnp.cos(ang), jnp.sin(ang)
    kind = lane_kind[None, :]
    c = jnp.where(kind == 2, 1.0, cos)
    s1 = jnp.where(kind == 0, -sin, 0.0)
    s2 = jnp.where(kind == 1, sin, 0.0)
    return jnp.stack([c, s1, s2]).astype(F32)


def _inv_freq(dim, theta):
    return theta ** (-jnp.arange(0, dim, 2, dtype=F32) / dim)


def _diff_tables(pos):
    lane = jnp.arange(LANES)
    within = lane % DIFF_QK_DIM
    half = DIFF_ROT // 2
    inv = _inv_freq(DIFF_ROT, ROPE_THETA)
    freq = inv[within % half]
    kind = jnp.where(within < half, 0, jnp.where(within < DIFF_ROT, 1, 2))
    return _rope_tables(pos, freq, kind)


def _mla_tables(pos):
    lane = jnp.arange(LANES)
    half = MLA_ROPE // 2
    inv = _inv_freq(MLA_ROPE, MLA_THETA)
    freq = inv[lane % half]
    kind = jnp.where(lane < half, 0, jnp.where(lane < MLA_ROPE, 1, 2))
    return _rope_tables(pos, freq, kind)


def _axial_tables(pos):
    lane = jnp.arange(LANES)
    hd = HEAD_DIM // 2
    half = hd // 2
    inv = _inv_freq(hd, AXIAL_THETA)
    freq = inv[lane % half]
    kind = jnp.where((lane % hd) < half, 0, 1)
    p = jnp.where(lane[None, :] < hd, (pos // GRID_W)[:, None], (pos % GRID_W)[:, None])
    return _rope_tables(p.astype(F32), freq, kind)


def _apply_rope(x, c, s1, s2, half):
    return x * c + pltpu.roll(x, LANES - half, 1) * s1 + pltpu.roll(x, half, 1) * s2


def _even_proj_kernel(x_ref, w_ref, tab_ref, o_ref, *, half):
    j = pl.program_id(0)
    acc = jnp.dot(x_ref[...], w_ref[...], preferred_element_type=F32)

    @pl.when(j < 2)
    def _():
        c, s1, s2 = tab_ref[0, 0], tab_ref[0, 1], tab_ref[0, 2]
        for ci in range(acc.shape[1] // LANES):
            sl = slice(ci * LANES, (ci + 1) * LANES)
            o_ref[:, sl] = _apply_rope(acc[:, sl], c, s1, s2, half).astype(o_ref.dtype)

    @pl.when(j >= 2)
    def _():
        o_ref[...] = acc.astype(o_ref.dtype)


def _even_proj(xb, w, tabs):
    t, d = xb.shape
    n = w.shape[1]
    tm, tn = PROJ_TM, PROJ_TN
    assert t % tm == 0 and n % tn == 0 and DIFF_HEADS * HEAD_DIM == tn
    return pl.pallas_call(
        functools.partial(_even_proj_kernel, half=DIFF_ROT // 2),
        grid=(n // tn, t // tm),
        in_specs=[
            pl.BlockSpec((tm, d), lambda j, i: (i, 0)),
            pl.BlockSpec((d, tn), lambda j, i: (0, j)),
            pl.BlockSpec((1, 3, tm, LANES), lambda j, i: (jnp.minimum(j, 1), 0, i, 0)),
        ],
        out_specs=pl.BlockSpec((tm, tn), lambda j, i: (i, j)),
        out_shape=jax.ShapeDtypeStruct((t, n), BF16),
        compiler_params=_cparams(("arbitrary", "arbitrary")),
        name="even_proj",
    )(xb, w, tabs)


def _plain_proj_kernel(x_ref, w_ref, o_ref):
    o_ref[...] = jnp.dot(x_ref[...], w_ref[...], preferred_element_type=F32).astype(o_ref.dtype)


def _plain_proj(xb, w):
    t, d = xb.shape
    n = w.shape[1]
    tm = PROJ_TM
    assert t % tm == 0 and n % LANES == 0
    return pl.pallas_call(
        _plain_proj_kernel,
        grid=(t // tm,),
        in_specs=[pl.BlockSpec((tm, d), lambda i: (i, 0)),
                  pl.BlockSpec((d, n), lambda i: (0, 0))],
        out_specs=pl.BlockSpec((tm, n), lambda i: (i, 0)),
        out_shape=jax.ShapeDtypeStruct((t, n), BF16),
        compiler_params=_cparams(("arbitrary",)),
        name="odd_proj",
    )(xb, w)


def _flash_loop(q, k_ref, v_ref, tk):
    s_len = k_ref.shape[0]
    r = q.shape[0]
    dv = v_ref.shape[1]

    def body(j, carry):
        m, l, acc = carry
        off = pl.multiple_of(j * tk, tk)
        kj = k_ref[pl.ds(off, tk), :]
        vj = v_ref[pl.ds(off, tk), :]
        s = lax.dot_general(q, kj, (((1,), (1,)), ((), ())), preferred_element_type=F32)
        m_new = jnp.maximum(m, jnp.max(s, axis=1, keepdims=True))
        a = jnp.exp2(m - m_new)
        p = jnp.exp2(s - m_new)
        l = a * l + jnp.sum(p, axis=1, keepdims=True)
        acc = a * acc + jnp.dot(p.astype(BF16), vj, preferred_element_type=F32)
        return m_new, l, acc

    init = (jnp.full((r, 1), -jnp.inf, F32), jnp.zeros((r, 1), F32), jnp.zeros((r, dv), F32))
    _, l, acc = lax.fori_loop(0, s_len // tk, body, init)
    return acc, l


def _diff_attn_kernel(lam_ref, g_ref, q_ref, k_ref, v_ref, o_ref, *, tk, lambda_init):
    tq = q_ref.shape[0]
    q = q_ref[...]
    lane = lax.broadcasted_iota(jnp.int32, q.shape, 1)
    zero = jnp.zeros_like(q)
    q2 = jnp.concatenate([jnp.where(lane < DIFF_QK_DIM, q, zero),
                          jnp.where(lane >= DIFF_QK_DIM, q, zero)], axis=0)
    acc, l = _flash_loop(q2, k_ref, v_ref, tk)
    o = acc / l
    lv = lam_ref[...].astype(F32)
    lam = (jnp.exp(jnp.sum(lv[0:1] * lv[1:2], axis=1, keepdims=True))
           - jnp.exp(jnp.sum(lv[2:3] * lv[3:4], axis=1, keepdims=True)) + lambda_init)
    o = o[:tq] - lam * o[tq:]
    y = o * lax.rsqrt(jnp.mean(o * o, axis=1, keepdims=True) + SUBLN_EPS)
    o_ref[...] = (y * g_ref[...] * (1.0 - lambda_init)).astype(o_ref.dtype)


def _diff_attention(h, lam_vec, subln_g, row0, batch, seq, lambda_init):
    tq = FLASH_ROWS // 2
    tk = min(FLASH_TK, seq)
    assert seq % tq == 0 and seq % tk == 0 and row0 % seq == 0
    nq = seq // tq
    hcols = DIFF_HEADS
    return pl.pallas_call(
        functools.partial(_diff_attn_kernel, tk=tk, lambda_init=lambda_init),
        grid=(batch, DIFF_HEADS, nq),
        in_specs=[
            pl.BlockSpec((4, DIFF_QK_DIM), lambda b, hh, i: (0, 0)),
            pl.BlockSpec((1, HEAD_DIM), lambda b, hh, i: (0, 0)),
            pl.BlockSpec((tq, HEAD_DIM), lambda b, hh, i: (row0 // tq + b * nq + i, hh)),
            pl.BlockSpec((seq, HEAD_DIM), lambda b, hh, i: (row0 // seq + b, hcols + hh)),
            pl.BlockSpec((seq, HEAD_DIM), lambda b, hh, i: (row0 // seq + b, 2 * hcols + hh)),
        ],
        out_specs=pl.BlockSpec((tq, HEAD_DIM), lambda b, hh, i: (b * nq + i, hh)),
        out_shape=jax.ShapeDtypeStruct((batch * seq, DIFF_HEADS * HEAD_DIM), BF16),
        compiler_params=_cparams(("arbitrary",) * 3),
        name="diff_attn",
    )(lam_vec, subln_g.reshape(1, HEAD_DIM), h, h, h)


def _mla_attn_kernel(q_ref, k_ref, v_ref, o_ref, *, tk):
    acc, l = _flash_loop(q_ref[...], k_ref, v_ref, tk)
    o_ref[...] = (acc / l).astype(o_ref.dtype)


def _mla_attention(qm, km, vm, row0, batch, seq):
    tq = FLASH_ROWS
    tk = min(FLASH_TK, seq)
    assert seq % tq == 0 and seq % tk == 0 and row0 % seq == 0
    nq = seq // tq
    dqk = 2 * LANES
    return pl.pallas_call(
        functools.partial(_mla_attn_kernel, tk=tk),
        grid=(batch, MLA_HEADS, nq),
        in_specs=[
            pl.BlockSpec((tq, dqk), lambda b, hh, i: (row0 // tq + b * nq + i, hh)),
            pl.BlockSpec((seq, dqk), lambda b, hh, i: (row0 // seq + b, hh)),
            pl.BlockSpec((seq, MLA_V), lambda b, hh, i: (row0 // seq + b, hh)),
        ],
        out_specs=pl.BlockSpec((tq, MLA_V), lambda b, hh, i: (b * nq + i, hh)),
        out_shape=jax.ShapeDtypeStruct((batch * seq, MLA_HEADS * MLA_V), BF16),
        compiler_params=_cparams(("arbitrary",) * 3),
        name="mla_attn",
    )(qm, km, vm)


def _gqa_attn_kernel(q_ref, k_ref, v_ref, o_ref, *, tk, group):
    tq = q_ref.shape[0]
    q = q_ref[...]
    q4 = jnp.concatenate([q[:, g * HEAD_DIM:(g + 1) * HEAD_DIM] for g in range(group)], axis=0)
    acc, l = _flash_loop(q4, k_ref, v_ref, tk)
    o = acc / l
    for g in range(group):
        o_ref[:, g * HEAD_DIM:(g + 1) * HEAD_DIM] = o[g * tq:(g + 1) * tq].astype(o_ref.dtype)


def _gqa_attention(qg, kg, h, v_col0, row0, batch, seq):
    group = GQA_HEADS // GQA_KV_HEADS
    tq = FLASH_ROWS // group
    tk = min(FLASH_TK, seq)
    assert seq % tq == 0 and seq % tk == 0 and row0 % seq == 0
    nq = seq // tq
    return pl.pallas_call(
        functools.partial(_gqa_attn_kernel, tk=tk, group=group),
        grid=(batch, GQA_KV_HEADS, nq),
        in_specs=[
            pl.BlockSpec((tq, group * HEAD_DIM), lambda b, n, i: (row0 // tq + b * nq + i, n)),
            pl.BlockSpec((seq, HEAD_DIM), lambda b, n, i: (row0 // seq + b, n)),
            pl.BlockSpec((seq, HEAD_DIM), lambda b, n, i: (row0 // seq + b, v_col0 + n)),
        ],
        out_specs=pl.BlockSpec((tq, group * HEAD_DIM), lambda b, n, i: (b * nq + i, n)),
        out_shape=jax.ShapeDtypeStruct((batch * seq, GQA_HEADS * HEAD_DIM), BF16),
        compiler_params=_cparams(("arbitrary",) * 3),
        name="gqa_attn",
    )(qg, kg, h)


def _na_bias_table(rpb):
    cols = jnp.arange(GRID_W)
    start = jnp.clip(cols - NA_KW // 2, 0, GRID_W - NA_KW)
    kc = cols[None, :]
    valid = (kc >= start[:, None]) & (kc < start[:, None] + NA_KW)
    dc = jnp.clip(kc - cols[:, None] + (NA_KW - 1), 0, 2 * NA_KW - 2)
    tab = rpb.astype(F32)[:, :, dc]
    tab = jnp.where(valid[None, None], tab, NEG_BIG)
    return jnp.concatenate([tab[:, :-1], tab[:, 1:]], axis=-1)


def _na_kernel(q_ref, kp_ref, kc_ref, kn_ref, vp_ref, vc_ref, vn_ref, bias_ref, o_ref,
               kbuf, vbuf, *, rows, scale):
    i = pl.program_id(1)
    blk = NA_ROWS * GRID_W
    kbuf[0:blk] = kp_ref[...]
    kbuf[blk:2 * blk] = kc_ref[...]
    kbuf[2 * blk:3 * blk] = kn_ref[...]
    vbuf[0:blk] = vp_ref[...]
    vbuf[blk:2 * blk] = vc_ref[...]
    vbuf[2 * blk:3 * blk] = vn_ref[...]
    win = NA_KH * GRID_W

    def row_body(j, _):
        r = i * NA_ROWS + j
        rs = jnp.clip(r - NA_KH // 2, 0, rows - NA_KH)
        dr0 = rs - r + (NA_KH - 1)
        koff = pl.multiple_of((rs - (i - 1) * NA_ROWS) * GRID_W, GRID_W)
        qoff = pl.multiple_of(j * GRID_W, GRID_W)
        for hh in range(NA_HEADS):
            cs = slice(hh * HEAD_DIM, (hh + 1) * HEAD_DIM)
            q = (q_ref[pl.ds(qoff, GRID_W), cs].astype(F32) * scale).astype(BF16)
            kw = kbuf[pl.ds(koff, win), cs]
            vw = vbuf[pl.ds(koff, win), cs]
            s = lax.dot_general(q, kw, (((1,), (1,)), ((), ())), preferred_element_type=F32)
            s = s + jnp.concatenate(
                [bias_ref[hh, dr0 + 2 * c] for c in range(win // LANES)], axis=1) * LOG2E
            m = jnp.max(s, axis=1, keepdims=True)
            p = jnp.exp2(s - m)
            l = jnp.sum(p, axis=1, keepdims=True)
            o = jnp.dot(p.astype(BF16), vw, preferred_element_type=F32) / l
            o_ref[pl.ds(qoff, GRID_W), cs] = o.astype(o_ref.dtype)
        return 0

    lax.fori_loop(0, NA_ROWS, row_body, 0)


def _na_attention(h, bias_tab, col0, row0, batch, seq):
    rows = seq // GRID_W
    assert seq % GRID_W == 0 and rows % NA_ROWS == 0 and rows >= NA_KH
    nb = rows // NA_ROWS
    blk = NA_ROWS * GRID_W
    width = NA_HEADS * HEAD_DIM
    assert row0 % blk == 0
    base = row0 // blk

    def cur(c):
        return lambda b, i: (base + b * nb + i, c)

    def prev(c):
        return lambda b, i: (base + b * nb + jnp.maximum(i - 1, 0), c)

    def nxt(c):
        return lambda b, i: (base + b * nb + jnp.minimum(i + 1, nb - 1), c)

    bspec = lambda f: pl.BlockSpec((blk, width), f)
    return pl.pallas_call(
        functools.partial(_na_kernel, rows=rows, scale=HEAD_DIM ** -0.5 * LOG2E),
        grid=(batch, nb),
        in_specs=[bspec(cur(col0)),
                  bspec(prev(col0 + 1)), bspec(cur(col0 + 1)), bspec(nxt(col0 + 1)),
                  bspec(prev(col0 + 2)), bspec(cur(col0 + 2)), bspec(nxt(col0 + 2)),
                  pl.BlockSpec(bias_tab.shape, lambda b, i: (0, 0, 0, 0))],
        out_specs=pl.BlockSpec((blk, width), lambda b, i: (b * nb + i, 0)),
        out_shape=jax.ShapeDtypeStruct((batch * seq, width), BF16),
        scratch_shapes=[pltpu.VMEM((3 * blk, width), BF16), pltpu.VMEM((3 * blk, width), BF16)],
        compiler_params=_cparams(("arbitrary",) * 2),
        name="na_attn",
    )(h, h, h, h, h, h, h, bias_tab)


def _rms(x, g, eps):
    return x * lax.rsqrt(jnp.mean(x * x, axis=1, keepdims=True) + eps) * g


def _odd_prep_kernel(h_ref, wq_ref, wk_ref, wv_ref, qn_ref, kvn_ref, gqn_ref, gkn_ref,
                     mtab_ref, atab_ref, qm_ref, km_ref, vm_ref, qg_ref, kg_ref, *, cols):
    c_qc, c_kvc, c_gq, c_gk, c_kpe = cols
    mla_scale = (MLA_NOPE + MLA_ROPE) ** -0.5 * LOG2E
    gqa_scale = HEAD_DIM ** -0.5 * LOG2E
    mc, ms1, ms2 = mtab_ref[0], mtab_ref[1], mtab_ref[2]
    ac, as1, as2 = atab_ref[0], atab_ref[1], atab_ref[2]
    mhalf = MLA_ROPE // 2
    ahalf = HEAD_DIM // 4

    qc = _rms(h_ref[:, c_qc:c_qc + MLA_Q_RANK].astype(F32), qn_ref[...], RMS_EPS)
    q = jnp.dot(qc.astype(BF16), wq_ref[...], preferred_element_type=F32)
    for hh in range(MLA_HEADS):
        lo = hh * 2 * LANES
        qm_ref[:, lo:lo + LANES] = (q[:, lo:lo + LANES] * mla_scale).astype(qm_ref.dtype)
        qr = _apply_rope(q[:, lo + LANES:lo + 2 * LANES], mc, ms1, ms2, mhalf)
        qm_ref[:, lo + LANES:lo + 2 * LANES] = (qr * mla_scale).astype(qm_ref.dtype)

    kvc = _rms(h_ref[:, c_kvc:c_kvc + MLA_KV_RANK].astype(F32), kvn_ref[...], RMS_EPS).astype(BF16)
    kn = jnp.dot(kvc, wk_ref[...], preferred_element_type=F32)
    vm_ref[...] = jnp.dot(kvc, wv_ref[...], preferred_element_type=F32).astype(vm_ref.dtype)
    kpe = _apply_rope(h_ref[:, c_kpe:c_kpe + LANES].astype(F32), mc, ms1, ms2, mhalf)
    kpe = kpe.astype(km_ref.dtype)
    for hh in range(MLA_HEADS):
        lo = hh * 2 * LANES
        km_ref[:, lo:lo + LANES] = kn[:, hh * LANES:(hh + 1) * LANES].astype(km_ref.dtype)
        km_ref[:, lo + LANES:lo + 2 * LANES] = kpe

    for hh in range(GQA_HEADS):
        sl = slice(c_gq + hh * HEAD_DIM, c_gq + (hh + 1) * HEAD_DIM)
        x = _rms(h_ref[:, sl].astype(F32), gqn_ref[...], RMS_EPS)
        x = _apply_rope(x, ac, as1, as2, ahalf) * gqa_scale
        qg_ref[:, hh * HEAD_DIM:(hh + 1) * HEAD_DIM] = x.astype(qg_ref.dtype)
    for hh in range(GQA_KV_HEADS):
        sl = slice(c_gk + hh * HEAD_DIM, c_gk + (hh + 1) * HEAD_DIM)
        x = _rms(h_ref[:, sl].astype(F32), gkn_ref[...], RMS_EPS)
        x = _apply_rope(x, ac, as1, as2, ahalf)
        kg_ref[:, hh * HEAD_DIM:(hh + 1) * HEAD_DIM] = x.astype(kg_ref.dtype)


def _odd_prep(h, wq, wk, wv, qn, kvn, gqn, gkn, mtab, atab, cols):
    t, n = h.shape
    tm = ROW_TM
    assert t % tm == 0
    row = lambda i: (i, 0)
    full = lambda i: (0, 0)
    tab = lambda i: (0, i, 0)
    widths = (MLA_HEADS * 2 * LANES, MLA_HEADS * 2 * LANES, MLA_HEADS * MLA_V,
              GQA_HEADS * HEAD_DIM, GQA_KV_HEADS * HEAD_DIM)
    return pl.pallas_call(
        functools.partial(_odd_prep_kernel, cols=cols),
        grid=(t // tm,),
        in_specs=[pl.BlockSpec((tm, n), row),
                  pl.BlockSpec(wq.shape, full), pl.BlockSpec(wk.shape, full), pl.BlockSpec(wv.shape, full),
                  pl.BlockSpec(qn.shape, full), pl.BlockSpec(kvn.shape, full),
                  pl.BlockSpec(gqn.shape, full), pl.BlockSpec(gkn.shape, full),
                  pl.BlockSpec((3, tm, LANES), tab), pl.BlockSpec((3, tm, LANES), tab)],
        out_specs=[pl.BlockSpec((tm, w), row) for w in widths],
        out_shape=[jax.ShapeDtypeStruct((t, w), BF16) for w in widths],
        compiler_params=_cparams(("arbitrary",)),
        name="odd_prep",
    )(h, wq, wk, wv, qn, kvn, gqn, gkn, mtab, atab)


def _layer_norm(y, g, b):
    mu = jnp.mean(y, axis=1, keepdims=True)
    yc = y - mu
    var = jnp.mean(yc * yc, axis=1, keepdims=True)
    return yc * lax.rsqrt(var + LN_EPS) * g + b


def _route(logits):
    lane = lax.broadcasted_iota(jnp.int32, logits.shape, 1)
    lanef = lane.astype(F32)
    big = float(LANES)
    gmask = lane < N_GROUPS
    gl = jnp.where(gmask, logits, -jnp.inf)
    gmax = jnp.max(gl, axis=1, keepdims=True)
    gsel = jnp.min(jnp.where(gl == gmax, lanef, big), axis=1, keepdims=True)
    gden = jnp.sum(jnp.where(gmask, jnp.exp(gl - gmax), 0.0), axis=1, keepdims=True)
    gprob = 1.0 / gden
    lo = N_GROUPS + EXPERTS_PER_GROUP * gsel
    emask = (lanef >= lo) & (lanef < lo + EXPERTS_PER_GROUP)
    el = jnp.where(emask, logits, -jnp.inf)
    v1 = jnp.max(el, axis=1, keepdims=True)
    i1 = jnp.min(jnp.where(el == v1, lanef, big), axis=1, keepdims=True)
    el2 = jnp.where(lanef == i1, -jnp.inf, el)
    v2 = jnp.max(el2, axis=1, keepdims=True)
    i2 = jnp.min(jnp.where(el2 == v2, lanef, big), axis=1, keepdims=True)
    e = jnp.exp(v2 - v1)
    g1 = gprob / (1.0 + e)
    g2 = gprob * e / (1.0 + e)
    eid = jnp.where(lane == 0, i1 - N_GROUPS, jnp.where(lane == 1, i2 - N_GROUPS, 0.0))
    gate = jnp.where(lane == 0, g1, jnp.where(lane == 1, g2, 0.0))
    return eid.astype(jnp.int32), gate


def _out_ln_route_kernel(a1_ref, a2_ref, w_ref, x_ref, g_ref, b_ref, wr_ref, br_ref,
                         x1_ref, eid_ref, gate_ref, *, alpha):
    k1 = a1_ref.shape[1]
    mix = (jnp.dot(a1_ref[...], w_ref[0:k1, :], preferred_element_type=F32)
           + jnp.dot(a2_ref[...], w_ref[k1:, :], preferred_element_type=F32))
    x1 = _layer_norm(alpha * x_ref[...] + mix, g_ref[...], b_ref[...])
    x1_ref[...] = x1
    logits = jnp.dot(x1, wr_ref[...], preferred_element_type=F32,
                     precision=lax.Precision.HIGHEST) + br_ref[...]
    eid, gate = _route(logits)
    eid_ref[...] = eid
    gate_ref[...] = gate


def _out_ln_route(a1, a2, w, x, g, b, wr, br, alpha):
    t, d = x.shape
    tm = ROW_TM
    assert t % tm == 0
    row = lambda i: (i, 0)
    full = lambda i: (0, 0)
    return pl.pallas_call(
        functools.partial(_out_ln_route_kernel, alpha=alpha),
        grid=(t // tm,),
        in_specs=[pl.BlockSpec((tm, a1.shape[1]), row), pl.BlockSpec((tm, a2.shape[1]), row),
                  pl.BlockSpec(w.shape, full), pl.BlockSpec((tm, d), row),
                  pl.BlockSpec((1, d), full), pl.BlockSpec((1, d), full),
                  pl.BlockSpec(wr.shape, full), pl.BlockSpec((1, LANES), full)],
        out_specs=[pl.BlockSpec((tm, d), row), pl.BlockSpec((tm, LANES), row),
                   pl.BlockSpec((tm, LANES), row)],
        out_shape=[jax.ShapeDtypeStruct((t, d), F32), jax.ShapeDtypeStruct((t, LANES), jnp.int32),
                   jax.ShapeDtypeStruct((t, LANES), F32)],
        compiler_params=_cparams(("arbitrary",)),
        name="out_ln_route",
    )(a1, a2, w, x, g, b, wr, br)


def _row_copy(src_ref, src_row, dst_ref, dst_row, sem):
    return pltpu.make_async_copy(src_ref.at[pl.ds(src_row, 1)], dst_ref.at[pl.ds(dst_row, 1)], sem)


def _dispatch_kernel(dest_ref, x_ref, xs_in_ref, xs_ref, sem):
    del xs_in_ref
    tm = x_ref.shape[0]

    def start(r, _):
        for k in range(TOP_K):
            _row_copy(x_ref, r, xs_ref, dest_ref[0, 0, TOP_K * r + k], sem).start()
        return 0

    def wait(r, _):
        for k in range(TOP_K):
            _row_copy(x_ref, r, xs_ref, dest_ref[0, 0, TOP_K * r + k], sem).wait()
        return 0

    lax.fori_loop(0, tm, start, 0)
    lax.fori_loop(0, tm, wait, 0)


def _dispatch(dest2d, x, xs_init):
    t, d = x.shape
    tm = ROW_TM
    return pl.pallas_call(
        _dispatch_kernel,
        grid=(t // tm,),
        in_specs=[pl.BlockSpec((1, 1, TOP_K * tm), lambda i: (i, 0, 0), memory_space=pltpu.SMEM),
                  pl.BlockSpec((tm, d), lambda i: (i, 0)),
                  pl.BlockSpec(memory_space=pl.ANY)],
        out_specs=pl.BlockSpec(memory_space=pl.ANY),
        out_shape=jax.ShapeDtypeStruct(xs_init.shape, xs_init.dtype),
        scratch_shapes=[pltpu.SemaphoreType.DMA(())],
        input_output_aliases={2: 0},
        compiler_params=_cparams(("arbitrary",)),
        name="moe_dispatch",
    )(dest2d, x, xs_init)


def _expert_kernel(be_ref, nv_ref, xs_ref, w1_ref, w3_ref, w2_ref, y_ref):
    del be_ref
    b = pl.program_id(0)

    @pl.when(b < nv_ref[0])
    def _():
        x = xs_ref[...].astype(BF16)
        h1 = jnp.dot(x, w1_ref[0], preferred_element_type=F32)
        h3 = jnp.dot(x, w3_ref[0], preferred_element_type=F32)
        hm = (h1 / (1.0 + jnp.exp(-h1)) * h3).astype(BF16)
        y_ref[...] = jnp.dot(hm, w2_ref[0], preferred_element_type=F32).astype(y_ref.dtype)

    @pl.when(b >= nv_ref[0])
    def _():
        y_ref[...] = jnp.zeros_like(y_ref)


def _experts(block_e, n_valid, xs, w1, w3, w2):
    p, d = xs.shape
    tb = MOE_TB
    nb = p // tb
    de = w1.shape[2]
    last = lambda b, nv: jnp.minimum(b, nv[0] - 1)
    return pl.pallas_call(
        _expert_kernel,
        grid_spec=pltpu.PrefetchScalarGridSpec(
            num_scalar_prefetch=2,
            grid=(nb,),
            in_specs=[pl.BlockSpec((tb, d), lambda b, be, nv: (last(b, nv), 0)),
                      pl.BlockSpec((1, d, de), lambda b, be, nv: (be[last(b, nv)], 0, 0)),
                      pl.BlockSpec((1, d, de), lambda b, be, nv: (be[last(b, nv)], 0, 0)),
                      pl.BlockSpec((1, de, d), lambda b, be, nv: (be[last(b, nv)], 0, 0))],
            out_specs=pl.BlockSpec((tb, d), lambda b, be, nv: (b, 0)),
        ),
        out_shape=jax.ShapeDtypeStruct((p, d), F32),
        compiler_params=_cparams(("arbitrary",)),
        name="moe_experts",
    )(block_e, n_valid, xs, w1, w3, w2)


def _combine_ln_kernel(dest_ref, gate_ref, x_ref, g_ref, b_ref, y_ref, o_ref, ob_ref, buf, sem, *, alpha):
    tm = x_ref.shape[0]

    def start(r, _):
        for k in range(TOP_K):
            _row_copy(y_ref, dest_ref[0, 0, TOP_K * r + k], buf.at[k], r, sem).start()
        return 0

    def wait(r, _):
        for k in range(TOP_K):
            _row_copy(y_ref, dest_ref[0, 0, TOP_K * r + k], buf.at[k], r, sem).wait()
        return 0

    lax.fori_loop(0, tm, start, 0)
    lax.fori_loop(0, tm, wait, 0)
    gate = gate_ref[...]
    ff = buf[0] * gate[:, 0:1] + buf[1] * gate[:, 1:2]
    x2 = _layer_norm(alpha * x_ref[...] + ff, g_ref[...], b_ref[...])
    o_ref[...] = x2
    ob_ref[...] = x2.astype(ob_ref.dtype)


def _combine_ln(dest2d, gate, x, g, b, y, alpha):
    t, d = x.shape
    tm = ROW_TM
    row = lambda i: (i, 0)
    full = lambda i: (0, 0)
    return pl.pallas_call(
        functools.partial(_combine_ln_kernel, alpha=alpha),
        grid=(t // tm,),
        in_specs=[pl.BlockSpec((1, 1, TOP_K * tm), lambda i: (i, 0, 0), memory_space=pltpu.SMEM),
                  pl.BlockSpec((tm, LANES), row), pl.BlockSpec((tm, d), row),
                  pl.BlockSpec((1, d), full), pl.BlockSpec((1, d), full),
                  pl.BlockSpec(memory_space=pl.ANY)],
        out_specs=[pl.BlockSpec((tm, d), row), pl.BlockSpec((tm, d), row)],
        out_shape=[jax.ShapeDtypeStruct((t, d), F32), jax.ShapeDtypeStruct((t, d), BF16)],
        scratch_shapes=[pltpu.VMEM((TOP_K, tm, d), F32), pltpu.SemaphoreType.DMA(())],
        compiler_params=_cparams(("arbitrary",)),
        name="moe_combine_ln",
    )(dest2d, gate, x, g, b, y)


def _moe_plan(eid, tb):
    t = eid.shape[0]
    a = t * TOP_K
    flat = eid.reshape(a)
    onehot = (flat[:, None] == jnp.arange(N_EXPERTS, dtype=jnp.int32)[None, :]).astype(jnp.int32)
    csum = jnp.cumsum(onehot, axis=0)
    rank = jnp.sum(csum * onehot, axis=1) - 1
    counts = csum[-1]
    padded = (counts + tb - 1) // tb * tb
    pend = jnp.cumsum(padded)
    pstart = pend - padded
    dest = (jnp.sum(pstart[None, :] * onehot, axis=1) + rank).astype(jnp.int32)
    nb = -(-a // tb) + N_EXPERTS
    block_e = jnp.minimum(jnp.searchsorted(pend, jnp.arange(nb, dtype=jnp.int32) * tb, side="right"),
                          N_EXPERTS - 1).astype(jnp.int32)
    n_valid = (pend[-1] // tb).astype(jnp.int32).reshape(1)
    return dest, block_e, n_valid, nb


def _segments(groups):
    segs, row0 = [], 0
    for b, s in groups:
        segs.append((row0, b, s))
        row0 += b * s
    return segs


def kernel(x_prompt, x_sample, ev_w_in, ev_w_out, diff_lambda, diff_subln, na_rpb, od_w_in, od_w_out,
           mla_q_norm, mla_w_q_up, mla_kv_norm, mla_w_kv_up, gqa_q_norm, gqa_k_norm,
           ln1_g, ln1_b, ln2_g, ln2_b, moe_w_group, moe_b_group, moe_w_expert, moe_b_expert,
           moe_w1, moe_w3, moe_w2):
    d = x_prompt.shape[-1]
    depth = ln1_g.shape[0]
    alpha = (2 * depth) ** 0.25
    groups = [(x_prompt.shape[0], x_prompt.shape[1]), (x_sample.shape[0], x_sample.shape[1])]
    segs = _segments(groups)
    x = jnp.concatenate([x_prompt.reshape(-1, d), x_sample.reshape(-1, d)], axis=0)
    t = x.shape[0]
    xb = x.astype(BF16)
    pos = jnp.concatenate([jnp.tile(jnp.arange(s, dtype=jnp.int32), b) for b, s in groups])
    posf = pos.astype(F32)

    for l in range(depth):
        i = l // 2
        if l % 2 == 0:
            tab = _diff_tables(posf)
            qscale = DIFF_QK_DIM ** -0.5 * LOG2E
            tabs = jnp.stack([tab * qscale, tab])
            h = _even_proj(xb, ev_w_in[i].astype(BF16), tabs)
            lambda_init = 0.8 - 0.6 * math.exp(-0.3 * l)
            bias_tab = _na_bias_table(na_rpb[i])
            o_a = jnp.concatenate(
                [_diff_attention(h, diff_lambda[i], diff_subln[i], r0, b, s, lambda_init)
                 for r0, b, s in segs], axis=0)
            o_b = jnp.concatenate(
                [_na_attention(h, bias_tab, 3, r0, b, s) for r0, b, s in segs], axis=0)
            w_out = ev_w_out[i]
        else:
            s1 = MLA_Q_RANK
            s2 = s1 + MLA_KV_RANK
            s3 = s2 + MLA_ROPE
            w_in = od_w_in[i]
            w_perm = jnp.concatenate(
                [w_in[:, :s2], w_in[:, s3:], w_in[:, s2:s3], jnp.zeros((d, LANES - MLA_ROPE), w_in.dtype)],
                axis=1).astype(BF16)
            c_gq = s2
            c_gk = c_gq + GQA_HEADS * HEAD_DIM
            c_gv = c_gk + GQA_KV_HEADS * HEAD_DIM
            c_kpe = c_gv + GQA_KV_HEADS * HEAD_DIM
            h = _plain_proj(xb, w_perm)
            wq = mla_w_q_up[i].reshape(MLA_Q_RANK, MLA_HEADS, MLA_NOPE + MLA_ROPE)
            wq = jnp.pad(wq, ((0, 0), (0, 0), (0, 2 * LANES - MLA_NOPE - MLA_ROPE)))
            wq = wq.reshape(MLA_Q_RANK, MLA_HEADS * 2 * LANES).astype(BF16)
            wkv = mla_w_kv_up[i].reshape(MLA_KV_RANK, MLA_HEADS, MLA_NOPE + MLA_V)
            wk = wkv[:, :, :MLA_NOPE].reshape(MLA_KV_RANK, MLA_HEADS * MLA_NOPE).astype(BF16)
            wv = wkv[:, :, MLA_NOPE:].reshape(MLA_KV_RANK, MLA_HEADS * MLA_V).astype(BF16)
            qm, km, vm, qg, kg = _odd_prep(
                h, wq, wk, wv, mla_q_norm[i].reshape(1, -1), mla_kv_norm[i].reshape(1, -1),
                gqa_q_norm[i].reshape(1, -1), gqa_k_norm[i].reshape(1, -1),
                _mla_tables(posf), _axial_tables(pos), (0, s1, c_gq, c_gk, c_kpe))
            o_a = jnp.concatenate(
                [_mla_attention(qm, km, vm, r0, b, s) for r0, b, s in segs], axis=0)
            o_b = jnp.concatenate(
                [_gqa_attention(qg, kg, h, c_gv // HEAD_DIM, r0, b, s) for r0, b, s in segs], axis=0)
            w_out = od_w_out[i]

        wr = jnp.concatenate([moe_w_group[l], moe_w_expert[l]], axis=1)
        wr = jnp.pad(wr, ((0, 0), (0, LANES - wr.shape[1])))
        br = jnp.concatenate([moe_b_group[l], moe_b_expert[l]])
        br = jnp.pad(br, (0, LANES - br.shape[0])).reshape(1, LANES)
        x1, eid, gate = _out_ln_route(o_a, o_b, w_out.astype(BF16), x, ln1_g[l].reshape(1, d),
                                      ln1_b[l].reshape(1, d), wr, br, alpha)

        dest, block_e, n_valid, nb = _moe_plan(eid[:, :TOP_K], MOE_TB)
        dest2d = dest.reshape(t // ROW_TM, 1, TOP_K * ROW_TM)
        xs = _dispatch(dest2d, x1, jnp.zeros((nb * MOE_TB, d), F32))
        y = _experts(block_e, n_valid, xs, moe_w1[l].astype(BF16), moe_w3[l].astype(BF16),
                     moe_w2[l].astype(BF16))
        x, xb = _combine_ln(dest2d, gate, x1, ln2_g[l].reshape(1, d), ln2_b[l].reshape(1, d), y, alpha)

    n_p = groups[0][0] * groups[0][1]
    return (x[:n_p].reshape(x_prompt.shape), x[n_p:].reshape(x_sample.shape))
```

```python
import functools
import math

import jax
import jax.numpy as jnp
from jax import lax
from jax.experimental import pallas as pl
from jax.experimental.pallas import tpu as pltpu

F32 = jnp.float32
BF16 = jnp.bfloat16

GRID_W = 64
HEAD_DIM = 128
ROPE_THETA = 500000.0
DIFF_HEADS = 8
DIFF_QK_DIM = HEAD_DIM // 2
DIFF_ROT = DIFF_QK_DIM // 4
NA_HEADS = 8
NA_KH = 8
NA_KW = 16
MLA_HEADS = 8
MLA_Q_RANK = 512
MLA_KV_RANK = 256
MLA_NOPE = 128
MLA_ROPE = 64
MLA_V = 128
MLA_THETA = 10000.0
GQA_HEADS = 8
GQA_KV_HEADS = 2
AXIAL_THETA = 10000.0
N_GROUPS = 4
EXPERTS_PER_GROUP = 8
N_EXPERTS = N_GROUPS * EXPERTS_PER_GROUP
TOP_K = 2
LN_EPS = 1e-5
RMS_EPS = 1e-6
SUBLN_EPS = 1e-5

LANES = 128
V7X_VMEM_BYTES = 64 * 1024 * 1024
VMEM_LIMIT = V7X_VMEM_BYTES * 7 // 8

LOG2E = math.log2(math.e)
NEG_BIG = -1e30

PROJ_TM = 512
PROJ_TN = 1024
ROW_TM = 256
FLASH_ROWS = 512
FLASH_TK = 1024
FLASH_UNROLL = 4
NA_ROWS = 8
MOE_TB = 256
DMA_UNROLL = 8


def _cparams(sem):
    return pltpu.CompilerParams(dimension_semantics=sem, vmem_limit_bytes=VMEM_LIMIT)


def _rope_tables(pos, lane_freq, lane_kind):
    if pos.ndim == 1:
        pos = pos[:, None]
    ang = pos * lane_freq[None, :]
    cos, sin = jnp.cos(ang), jnp.sin(ang)
    kind = lane_kind[None, :]
    c = jnp.where(kind == 2, 1.0, cos)
    s1 = jnp.where(kind == 0, -sin, 0.0)
    s2 = jnp.where(kind == 1, sin, 0.0)
    return jnp.stack([c, s1, s2]).astype(F32)


def _inv_freq(dim, theta):
    return theta ** (-jnp.arange(0, dim, 2, dtype=F32) / dim)


def _diff_tables(pos):
    lane = jnp.arange(LANES)
    within = lane % DIFF_QK_DIM
    half = DIFF_ROT // 2
    inv = _inv_freq(DIFF_ROT, ROPE_THETA)
    freq = inv[within % half]
    kind = jnp.where(within < half, 0, jnp.where(within < DIFF_ROT, 1, 2))
    return _rope_tables(pos, freq, kind)


def _mla_tables(pos):
    lane = jnp.arange(LANES)
    half = MLA_ROPE // 2
    inv = _inv_freq(MLA_ROPE, MLA_THETA)
    freq = inv[lane % half]
    kind = jnp.where(lane < half, 0, jnp.where(lane < MLA_ROPE, 1, 2))
    return _rope_tables(pos, freq, kind)


def _axial_tables(pos):
    lane = jnp.arange(LANES)
    hd = HEAD_DIM // 2
    half = hd // 2
    inv = _inv_freq(hd, AXIAL_THETA)
    freq = inv[lane % half]
    kind = jnp.where((lane % hd) < half, 0, 1)
    p = jnp.where(lane[None, :] < hd, (pos // GRID_W)[:, None], (pos % GRID_W)[:, None])
    return _rope_tables(p.astype(F32), freq, kind)


def _apply_rope(x, c, s1, s2, half):
    return x * c + pltpu.roll(x, LANES - half, 1) * s1 + pltpu.roll(x, half, 1) * s2


def _even_proj_kernel(x_ref, w_ref, tab_ref, o_ref, *, half):
    j = pl.program_id(0)
    acc = jnp.dot(x_ref[...], w_ref[...], preferred_element_type=F32)

    @pl.when(j < 2)
    def _():
        c, s1, s2 = tab_ref[0, 0], tab_ref[0, 1], tab_ref[0, 2]
        for ci in range(acc.shape[1] // LANES):
            sl = slice(ci * LANES, (ci + 1) * LANES)
            o_ref[:, sl] = _apply_rope(acc[:, sl], c, s1, s2, half).astype(o_ref.dtype)

    @pl.when(j >= 2)
    def _():
        o_ref[...] = acc.astype(o_ref.dtype)


def _even_proj(xb, w, tabs):
    t, d = xb.shape
    n = w.shape[1]
    tm, tn = PROJ_TM, PROJ_TN
    assert t % tm == 0 and n % tn == 0 and DIFF_HEADS * HEAD_DIM == tn
    return pl.pallas_call(
        functools.partial(_even_proj_kernel, half=DIFF_ROT // 2),
        grid=(n // tn, t // tm),
        in_specs=[
            pl.BlockSpec((tm, d), lambda j, i: (i, 0)),
            pl.BlockSpec((d, tn), lambda j, i: (0, j)),
            pl.BlockSpec((1, 3, tm, LANES), lambda j, i: (jnp.minimum(j, 1), 0, i, 0)),
        ],
        out_specs=pl.BlockSpec((tm, tn), lambda j, i: (i, j)),
        out_shape=jax.ShapeDtypeStruct((t, n), BF16),
        compiler_params=_cparams(("arbitrary", "arbitrary")),
        name="even_proj",
    )(xb, w, tabs)


def _plain_proj_kernel(x_ref, w_ref, o_ref):
    o_ref[...] = jnp.dot(x_ref[...], w_ref[...], preferred_element_type=F32).astype(o_ref.dtype)


def _plain_proj(xb, w):
    t, d = xb.shape
    n = w.shape[1]
    tm = PROJ_TM
    assert t % tm == 0 and n % LANES == 0
    return pl.pallas_call(
        _plain_proj_kernel,
        grid=(t // tm,),
        in_specs=[pl.BlockSpec((tm, d), lambda i: (i, 0)),
                  pl.BlockSpec((d, n), lambda i: (0, 0))],
        out_specs=pl.BlockSpec((tm, n), lambda i: (i, 0)),
        out_shape=jax.ShapeDtypeStruct((t, n), BF16),
        compiler_params=_cparams(("arbitrary",)),
        name="odd_proj",
    )(xb, w)


def _flash_scratch(rows, dqk, dv, tk):
    return [pltpu.VMEM((rows, dqk), BF16),
            pltpu.VMEM((2, rows, tk), F32),
            pltpu.VMEM((rows, LANES), F32),
            pltpu.VMEM((rows, LANES), F32),
            pltpu.VMEM((rows, dv), F32)]


def _flash_loop(q_ref, k_ref, v_ref, s_ref, m_ref, l_ref, acc_ref, tk):
    n = k_ref.shape[0] // tk
    unroll = FLASH_UNROLL if n % FLASH_UNROLL == 0 else 2
    assert n % unroll == 0
    nc = tk // LANES

    def scores(j, slot):
        off = pl.multiple_of(j * tk, tk)
        s_ref[slot] = lax.dot_general(q_ref[...], k_ref[pl.ds(off, tk), :],
                                      (((1,), (1,)), ((), ())), preferred_element_type=F32)

    def softmax_pv(j, slot):
        off = pl.multiple_of(j * tk, tk)
        cmax = s_ref[slot, :, 0:LANES]
        for c in range(1, nc):
            cmax = jnp.maximum(cmax, s_ref[slot, :, c * LANES:(c + 1) * LANES])
        m_prev = m_ref[...]
        m_new = jnp.maximum(m_prev, jnp.max(cmax, axis=1, keepdims=True))
        a = jnp.exp2(m_prev - m_new)
        ps = [jnp.exp2(s_ref[slot, :, c * LANES:(c + 1) * LANES] - m_new) for c in range(nc)]
        psum = ps[0]
        for c in range(1, nc):
            psum = psum + ps[c]
        l_ref[...] = a * l_ref[...] + psum
        p = jnp.concatenate([x.astype(BF16) for x in ps], axis=1)
        acc_ref[...] = a * acc_ref[...] + jnp.dot(p, v_ref[pl.ds(off, tk), :],
                                                  preferred_element_type=F32)
        m_ref[...] = m_new

    m_ref[...] = jnp.full(m_ref.shape, -jnp.inf, F32)
    l_ref[...] = jnp.zeros(l_ref.shape, F32)
    acc_ref[...] = jnp.zeros(acc_ref.shape, F32)
    scores(0, 0)

    def trip(i, _):
        for u in range(unroll):
            scores(unroll * i + u + 1, (u + 1) % 2)
            softmax_pv(unroll * i + u, u % 2)
        return 0

    lax.fori_loop(0, n // unroll - 1, trip, 0)
    for u in range(unroll):
        if u + 1 < unroll:
            scores(n - unroll + u + 1, (u + 1) % 2)
        softmax_pv(n - unroll + u, u % 2)
    return acc_ref[...], jnp.sum(l_ref[...], axis=1, keepdims=True)


def _flash_tk(seq):
    return min(FLASH_TK, seq // 2)


def _diff_attn_kernel(lam_ref, g_ref, q_ref, k_ref, v_ref, o_ref, q_scr, *scr, tk, lambda_init):
    tq = q_ref.shape[0]
    q = q_ref[...]
    lane = lax.broadcasted_iota(jnp.int32, q.shape, 1)
    zero = jnp.zeros_like(q)
    q_scr[0:tq] = jnp.where(lane < DIFF_QK_DIM, q, zero)
    q_scr[tq:2 * tq] = jnp.where(lane >= DIFF_QK_DIM, q, zero)
    acc, l = _flash_loop(q_scr, k_ref, v_ref, *scr, tk)
    o = acc / l
    lv = lam_ref[...].astype(F32)
    lam = (jnp.exp(jnp.sum(lv[0:1] * lv[1:2], axis=1, keepdims=True))
           - jnp.exp(jnp.sum(lv[2:3] * lv[3:4], axis=1, keepdims=True)) + lambda_init)
    o = o[:tq] - lam * o[tq:]
    y = o * lax.rsqrt(jnp.mean(o * o, axis=1, keepdims=True) + SUBLN_EPS)
    o_ref[...] = (y * g_ref[...] * (1.0 - lambda_init)).astype(o_ref.dtype)


def _diff_attention(h, lam_vec, subln_g, row0, batch, seq, lambda_init):
    tq = FLASH_ROWS // 2
    tk = _flash_tk(seq)
    assert seq % tq == 0 and seq % tk == 0 and row0 % seq == 0
    nq = seq // tq
    hcols = DIFF_HEADS
    return pl.pallas_call(
        functools.partial(_diff_attn_kernel, tk=tk, lambda_init=lambda_init),
        grid=(batch, DIFF_HEADS, nq),
        in_specs=[
            pl.BlockSpec((4, DIFF_QK_DIM), lambda b, hh, i: (0, 0)),
            pl.BlockSpec((1, HEAD_DIM), lambda b, hh, i: (0, 0)),
            pl.BlockSpec((tq, HEAD_DIM), lambda b, hh, i: (row0 // tq + b * nq + i, hh)),
            pl.BlockSpec((seq, HEAD_DIM), lambda b, hh, i: (row0 // seq + b, hcols + hh)),
            pl.BlockSpec((seq, HEAD_DIM), lambda b, hh, i: (row0 // seq + b, 2 * hcols + hh)),
        ],
        out_specs=pl.BlockSpec((tq, HEAD_DIM), lambda b, hh, i: (b * nq + i, hh)),
        out_shape=jax.ShapeDtypeStruct((batch * seq, DIFF_HEADS * HEAD_DIM), BF16),
        scratch_shapes=_flash_scratch(2 * tq, HEAD_DIM, HEAD_DIM, tk),
        compiler_params=_cparams(("arbitrary",) * 3),
        name="diff_attn",
    )(lam_vec, subln_g.reshape(1, HEAD_DIM), h, h, h)


def _mla_attn_kernel(q_ref, k_ref, v_ref, o_ref, q_scr, *scr, tk):
    del q_scr
    acc, l = _flash_loop(q_ref, k_ref, v_ref, *scr, tk)
    o_ref[...] = (acc / l).astype(o_ref.dtype)


def _mla_attention(qm, km, vm, row0, batch, seq):
    tq = FLASH_ROWS
    tk = _flash_tk(seq)
    assert seq % tq == 0 and seq % tk == 0 and row0 % seq == 0
    nq = seq // tq
    dqk = 2 * LANES
    return pl.pallas_call(
        functools.partial(_mla_attn_kernel, tk=tk),
        grid=(batch, MLA_HEADS, nq),
        in_specs=[
            pl.BlockSpec((tq, dqk), lambda b, hh, i: (row0 // tq + b * nq + i, hh)),
            pl.BlockSpec((seq, dqk), lambda b, hh, i: (row0 // seq + b, hh)),
            pl.BlockSpec((seq, MLA_V), lambda b, hh, i: (row0 // seq + b, hh)),
        ],
        out_specs=pl.BlockSpec((tq, MLA_V), lambda b, hh, i: (b * nq + i, hh)),
        out_shape=jax.ShapeDtypeStruct((batch * seq, MLA_HEADS * MLA_V), BF16),
        scratch_shapes=_flash_scratch(tq, 8, MLA_V, tk),
        compiler_params=_cparams(("arbitrary",) * 3),
        name="mla_attn",
    )(qm, km, vm)


def _gqa_attn_kernel(q_ref, k_ref, v_ref, o_ref, q_scr, *scr, tk, group):
    tq = q_ref.shape[0]
    for g in range(group):
        q_scr[g * tq:(g + 1) * tq] = q_ref[:, g * HEAD_DIM:(g + 1) * HEAD_DIM]
    acc, l = _flash_loop(q_scr, k_ref, v_ref, *scr, tk)
    o = acc / l
    for g in range(group):
        o_ref[:, g * HEAD_DIM:(g + 1) * HEAD_DIM] = o[g * tq:(g + 1) * tq].astype(o_ref.dtype)


def _gqa_attention(qg, kg, h, v_col0, row0, batch, seq):
    group = GQA_HEADS // GQA_KV_HEADS
    tq = FLASH_ROWS // group
    tk = _flash_tk(seq)
    assert seq % tq == 0 and seq % tk == 0 and row0 % seq == 0
    nq = seq // tq
    return pl.pallas_call(
        functools.partial(_gqa_attn_kernel, tk=tk, group=group),
        grid=(batch, GQA_KV_HEADS, nq),
        in_specs=[
            pl.BlockSpec((tq, group * HEAD_DIM), lambda b, n, i: (row0 // tq + b * nq + i, n)),
            pl.BlockSpec((seq, HEAD_DIM), lambda b, n, i: (row0 // seq + b, n)),
            pl.BlockSpec((seq, HEAD_DIM), lambda b, n, i: (row0 // seq + b, v_col0 + n)),
        ],
        out_specs=pl.BlockSpec((tq, group * HEAD_DIM), lambda b, n, i: (b * nq + i, n)),
        out_shape=jax.ShapeDtypeStruct((batch * seq, GQA_HEADS * HEAD_DIM), BF16),
        scratch_shapes=_flash_scratch(group * tq, HEAD_DIM, HEAD_DIM, tk),
        compiler_params=_cparams(("arbitrary",) * 3),
        name="gqa_attn",
    )(qg, kg, h)


def _na_bias_table(rpb):
    cols = jnp.arange(GRID_W)
    start = jnp.clip(cols - NA_KW // 2, 0, GRID_W - NA_KW)
    kc = cols[None, :]
    valid = (kc >= start[:, None]) & (kc < start[:, None] + NA_KW)
    dc = jnp.clip(kc - cols[:, None] + (NA_KW - 1), 0, 2 * NA_KW - 2)
    tab = rpb.astype(F32)[:, :, dc]
    tab = jnp.where(valid[None, None], tab, NEG_BIG)
    return jnp.concatenate([tab[:, :-1], tab[:, 1:]], axis=-1)


def _na_kernel(q_ref, kp_ref, kc_ref, kn_ref, vp_ref, vc_ref, vn_ref, bias_ref, o_ref,
               kbuf, vbuf, *, rows, scale):
    i = pl.program_id(1)
    blk = NA_ROWS * GRID_W
    kbuf[0:blk] = kp_ref[...]
    kbuf[blk:2 * blk] = kc_ref[...]
    kbuf[2 * blk:3 * blk] = kn_ref[...]
    vbuf[0:blk] = vp_ref[...]
    vbuf[blk:2 * blk] = vc_ref[...]
    vbuf[2 * blk:3 * blk] = vn_ref[...]
    win = NA_KH * GRID_W

    def row_body(j, _):
        r = i * NA_ROWS + j
        rs = jnp.clip(r - NA_KH // 2, 0, rows - NA_KH)
        dr0 = rs - r + (NA_KH - 1)
        koff = pl.multiple_of((rs - (i - 1) * NA_ROWS) * GRID_W, GRID_W)
        qoff = pl.multiple_of(j * GRID_W, GRID_W)
        for hh in range(NA_HEADS):
            cs = slice(hh * HEAD_DIM, (hh + 1) * HEAD_DIM)
            q = (q_ref[pl.ds(qoff, GRID_W), cs].astype(F32) * scale).astype(BF16)
            kw = kbuf[pl.ds(koff, win), cs]
            vw = vbuf[pl.ds(koff, win), cs]
            s = lax.dot_general(q, kw, (((1,), (1,)), ((), ())), preferred_element_type=F32)
            s = s + jnp.concatenate(
                [bias_ref[hh, dr0 + 2 * c] for c in range(win // LANES)], axis=1) * LOG2E
            m = jnp.max(s, axis=1, keepdims=True)
            p = jnp.exp2(s - m)
            l = jnp.sum(p, axis=1, keepdims=True)
            o = jnp.dot(p.astype(BF16), vw, preferred_element_type=F32) / l
            o_ref[pl.ds(qoff, GRID_W), cs] = o.astype(o_ref.dtype)
        return 0

    lax.fori_loop(0, NA_ROWS, row_body, 0)


def _na_attention(h, bias_tab, col0, row0, batch, seq):
    rows = seq // GRID_W
    assert seq % GRID_W == 0 and rows % NA_ROWS == 0 and rows >= NA_KH
    nb = rows // NA_ROWS
    blk = NA_ROWS * GRID_W
    width = NA_HEADS * HEAD_DIM
    assert row0 % blk == 0
    base = row0 // blk

    def cur(c):
        return lambda b, i: (base + b * nb + i, c)

    def prev(c):
        return lambda b, i: (base + b * nb + jnp.maximum(i - 1, 0), c)

    def nxt(c):
        return lambda b, i: (base + b * nb + jnp.minimum(i + 1, nb - 1), c)

    bspec = lambda f: pl.BlockSpec((blk, width), f)
    return pl.pallas_call(
        functools.partial(_na_kernel, rows=rows, scale=HEAD_DIM ** -0.5 * LOG2E),
        grid=(batch, nb),
        in_specs=[bspec(cur(col0)),
                  bspec(prev(col0 + 1)), bspec(cur(col0 + 1)), bspec(nxt(col0 + 1)),
                  bspec(prev(col0 + 2)), bspec(cur(col0 + 2)), bspec(nxt(col0 + 2)),
                  pl.BlockSpec(bias_tab.shape, lambda b, i: (0, 0, 0, 0))],
        out_specs=pl.BlockSpec((blk, width), lambda b, i: (b * nb + i, 0)),
        out_shape=jax.ShapeDtypeStruct((batch * seq, width), BF16),
        scratch_shapes=[pltpu.VMEM((3 * blk, width), BF16), pltpu.VMEM((3 * blk, width), BF16)],
        compiler_params=_cparams(("arbitrary",) * 2),
        name="na_attn",
    )(h, h, h, h, h, h, h, bias_tab)


def _rms(x, g, eps):
    return x * lax.rsqrt(jnp.mean(x * x, axis=1, keepdims=True) + eps) * g


def _odd_prep_kernel(h_ref, wq_ref, wk_ref, wv_ref, qn_ref, kvn_ref, gqn_ref, gkn_ref,
                     mtab_ref, atab_ref, qm_ref, km_ref, vm_ref, qg_ref, kg_ref, *, cols):
    c_qc, c_kvc, c_gq, c_gk, c_kpe = cols
    mla_scale = (MLA_NOPE + MLA_ROPE) ** -0.5 * LOG2E
    gqa_scale = HEAD_DIM ** -0.5 * LOG2E
    mc, ms1, ms2 = mtab_ref[0], mtab_ref[1], mtab_ref[2]
    ac, as1, as2 = atab_ref[0], atab_ref[1], atab_ref[2]
    mhalf = MLA_ROPE // 2
    ahalf = HEAD_DIM // 4

    qc = _rms(h_ref[:, c_qc:c_qc + MLA_Q_RANK].astype(F32), qn_ref[...], RMS_EPS)
    q = jnp.dot(qc.astype(BF16), wq_ref[...], preferred_element_type=F32)
    for hh in range(MLA_HEADS):
        lo = hh * 2 * LANES
        qm_ref[:, lo:lo + LANES] = (q[:, lo:lo + LANES] * mla_scale).astype(qm_ref.dtype)
        qr = _apply_rope(q[:, lo + LANES:lo + 2 * LANES], mc, ms1, ms2, mhalf)
        qm_ref[:, lo + LANES:lo + 2 * LANES] = (qr * mla_scale).astype(qm_ref.dtype)

    kvc = _rms(h_ref[:, c_kvc:c_kvc + MLA_KV_RANK].astype(F32), kvn_ref[...], RMS_EPS).astype(BF16)
    kn = jnp.dot(kvc, wk_ref[...], preferred_element_type=F32)
    vm_ref[...] = jnp.dot(kvc, wv_ref[...], preferred_element_type=F32).astype(vm_ref.dtype)
    kpe = _apply_rope(h_ref[:, c_kpe:c_kpe + LANES].astype(F32), mc, ms1, ms2, mhalf)
    kpe = kpe.astype(km_ref.dtype)
    for hh in range(MLA_HEADS):
        lo = hh * 2 * LANES
        km_ref[:, lo:lo + LANES] = kn[:, hh * LANES:(hh + 1) * LANES].astype(km_ref.dtype)
        km_ref[:, lo + LANES:lo + 2 * LANES] = kpe

    for hh in range(GQA_HEADS):
        sl = slice(c_gq + hh * HEAD_DIM, c_gq + (hh + 1) * HEAD_DIM)
        x = _rms(h_ref[:, sl].astype(F32), gqn_ref[...], RMS_EPS)
        x = _apply_rope(x, ac, as1, as2, ahalf) * gqa_scale
        qg_ref[:, hh * HEAD_DIM:(hh + 1) * HEAD_DIM] = x.astype(qg_ref.dtype)
    for hh in range(GQA_KV_HEADS):
        sl = slice(c_gk + hh * HEAD_DIM, c_gk + (hh + 1) * HEAD_DIM)
        x = _rms(h_ref[:, sl].astype(F32), gkn_ref[...], RMS_EPS)
        x = _apply_rope(x, ac, as1, as2, ahalf)
        kg_ref[:, hh * HEAD_DIM:(hh + 1) * HEAD_DIM] = x.astype(kg_ref.dtype)


def _odd_prep(h, wq, wk, wv, qn, kvn, gqn, gkn, mtab, atab, cols):
    t, n = h.shape
    tm = ROW_TM
    assert t % tm == 0
    row = lambda i: (i, 0)
    full = lambda i: (0, 0)
    tab = lambda i: (0, i, 0)
    widths = (MLA_HEADS * 2 * LANES, MLA_HEADS * 2 * LANES, MLA_HEADS * MLA_V,
              GQA_HEADS * HEAD_DIM, GQA_KV_HEADS * HEAD_DIM)
    return pl.pallas_call(
        functools.partial(_odd_prep_kernel, cols=cols),
        grid=(t // tm,),
        in_specs=[pl.BlockSpec((tm, n), row),
                  pl.BlockSpec(wq.shape, full), pl.BlockSpec(wk.shape, full), pl.BlockSpec(wv.shape, full),
                  pl.BlockSpec(qn.shape, full), pl.BlockSpec(kvn.shape, full),
                  pl.BlockSpec(gqn.shape, full), pl.BlockSpec(gkn.shape, full),
                  pl.BlockSpec((3, tm, LANES), tab), pl.BlockSpec((3, tm, LANES), tab)],
        out_specs=[pl.BlockSpec((tm, w), row) for w in widths],
        out_shape=[jax.ShapeDtypeStruct((t, w), BF16) for w in widths],
        compiler_params=_cparams(("arbitrary",)),
        name="odd_prep",
    )(h, wq, wk, wv, qn, kvn, gqn, gkn, mtab, atab)


def _layer_norm(y, g, b):
    mu = jnp.mean(y, axis=1, keepdims=True)
    yc = y - mu
    var = jnp.mean(yc * yc, axis=1, keepdims=True)
    return yc * lax.rsqrt(var + LN_EPS) * g + b


def _route(logits):
    lane = lax.broadcasted_iota(jnp.int32, logits.shape, 1)
    lanef = lane.astype(F32)
    big = float(LANES)
    gmask = lane < N_GROUPS
    gl = jnp.where(gmask, logits, -jnp.inf)
    gmax = jnp.max(gl, axis=1, keepdims=True)
    gsel = jnp.min(jnp.where(gl == gmax, lanef, big), axis=1, keepdims=True)
    gden = jnp.sum(jnp.where(gmask, jnp.exp(gl - gmax), 0.0), axis=1, keepdims=True)
    gprob = 1.0 / gden
    lo = N_GROUPS + EXPERTS_PER_GROUP * gsel
    emask = (lanef >= lo) & (lanef < lo + EXPERTS_PER_GROUP)
    el = jnp.where(emask, logits, -jnp.inf)
    v1 = jnp.max(el, axis=1, keepdims=True)
    i1 = jnp.min(jnp.where(el == v1, lanef, big), axis=1, keepdims=True)
    el2 = jnp.where(lanef == i1, -jnp.inf, el)
    v2 = jnp.max(el2, axis=1, keepdims=True)
    i2 = jnp.min(jnp.where(el2 == v2, lanef, big), axis=1, keepdims=True)
    e = jnp.exp(v2 - v1)
    g1 = gprob / (1.0 + e)
    g2 = gprob * e / (1.0 + e)
    eid = jnp.where(lane == 0, i1 - N_GROUPS, jnp.where(lane == 1, i2 - N_GROUPS, 0.0))
    gate = jnp.where(lane == 0, g1, jnp.where(lane == 1, g2, 0.0))
    return eid.astype(jnp.int32), gate


def _out_ln_route_kernel(a1_ref, a2_ref, w_ref, x_ref, g_ref, b_ref, wr_ref, br_ref,
                         x1_ref, eid_ref, gate_ref, *, alpha):
    k1 = a1_ref.shape[1]
    mix = (jnp.dot(a1_ref[...], w_ref[0:k1, :], preferred_element_type=F32)
           + jnp.dot(a2_ref[...], w_ref[k1:, :], preferred_element_type=F32))
    x1 = _layer_norm(alpha * x_ref[...] + mix, g_ref[...], b_ref[...])
    x1_ref[...] = x1
    x_hi = x1.astype(BF16)
    x_mid = (x1 - x_hi.astype(F32)).astype(BF16)
    t = (jnp.dot(x_hi, wr_ref[...], preferred_element_type=F32)
         + jnp.dot(x_mid, wr_ref[...], preferred_element_type=F32))
    logits = t[:, :LANES] + t[:, LANES:] + br_ref[...]
    eid, gate = _route(logits)
    eid_ref[...] = eid
    gate_ref[...] = gate


def _out_ln_route(a1, a2, w, x, g, b, wr, br, alpha):
    t, d = x.shape
    tm = ROW_TM
    assert t % tm == 0
    row = lambda i: (i, 0)
    full = lambda i: (0, 0)
    return pl.pallas_call(
        functools.partial(_out_ln_route_kernel, alpha=alpha),
        grid=(t // tm,),
        in_specs=[pl.BlockSpec((tm, a1.shape[1]), row), pl.BlockSpec((tm, a2.shape[1]), row),
                  pl.BlockSpec(w.shape, full), pl.BlockSpec((tm, d), row),
                  pl.BlockSpec((1, d), full), pl.BlockSpec((1, d), full),
                  pl.BlockSpec(wr.shape, full), pl.BlockSpec((1, LANES), full)],
        out_specs=[pl.BlockSpec((tm, d), row), pl.BlockSpec((tm, LANES), row),
                   pl.BlockSpec((tm, LANES), row)],
        out_shape=[jax.ShapeDtypeStruct((t, d), F32), jax.ShapeDtypeStruct((t, LANES), jnp.int32),
                   jax.ShapeDtypeStruct((t, LANES), F32)],
        compiler_params=_cparams(("arbitrary",)),
        name="out_ln_route",
    )(a1, a2, w, x, g, b, wr, br)


def _row_copy(src_ref, src_row, dst_ref, dst_row, sem):
    return pltpu.make_async_copy(src_ref.at[pl.ds(src_row, 1)], dst_ref.at[pl.ds(dst_row, 1)], sem)


def _dispatch_kernel(dest_ref, x_ref, xs_in_ref, xs_ref, sem):
    del xs_in_ref
    tm = x_ref.shape[0]

    def start(r, _):
        for k in range(TOP_K):
            _row_copy(x_ref, r, xs_ref, dest_ref[0, 0, TOP_K * r + k], sem).start()
        return 0

    def wait(r, _):
        for k in range(TOP_K):
            _row_copy(x_ref, r, xs_ref, dest_ref[0, 0, TOP_K * r + k], sem).wait()
        return 0

    lax.fori_loop(0, tm, start, 0, unroll=DMA_UNROLL)
    lax.fori_loop(0, tm, wait, 0, unroll=DMA_UNROLL)


def _dispatch(dest2d, x, xs_init):
    t, d = x.shape
    tm = ROW_TM
    return pl.pallas_call(
        _dispatch_kernel,
        grid=(t // tm,),
        in_specs=[pl.BlockSpec((1, 1, TOP_K * tm), lambda i: (i, 0, 0), memory_space=pltpu.SMEM),
                  pl.BlockSpec((tm, d), lambda i: (i, 0)),
                  pl.BlockSpec(memory_space=pl.ANY)],
        out_specs=pl.BlockSpec(memory_space=pl.ANY),
        out_shape=jax.ShapeDtypeStruct(xs_init.shape, xs_init.dtype),
        scratch_shapes=[pltpu.SemaphoreType.DMA(())],
        input_output_aliases={2: 0},
        compiler_params=_cparams(("arbitrary",)),
        name="moe_dispatch",
    )(dest2d, x, xs_init)


def _expert_kernel(be_ref, nv_ref, xs_ref, w1_ref, w3_ref, w2_ref, y_ref):
    del be_ref
    b = pl.program_id(0)

    @pl.when(b < nv_ref[0])
    def _():
        x = xs_ref[...].astype(BF16)
        h1 = jnp.dot(x, w1_ref[0], preferred_element_type=F32)
        h3 = jnp.dot(x, w3_ref[0], preferred_element_type=F32)
        hm = (h1 / (1.0 + jnp.exp(-h1)) * h3).astype(BF16)
        y_ref[...] = jnp.dot(hm, w2_ref[0], preferred_element_type=F32).astype(y_ref.dtype)

    @pl.when(b >= nv_ref[0])
    def _():
        y_ref[...] = jnp.zeros_like(y_ref)


def _experts(block_e, n_valid, xs, w1, w3, w2):
    p, d = xs.shape
    tb = MOE_TB
    nb = p // tb
    de = w1.shape[2]
    last = lambda b, nv: jnp.minimum(b, nv[0] - 1)
    return pl.pallas_call(
        _expert_kernel,
        grid_spec=pltpu.PrefetchScalarGridSpec(
            num_scalar_prefetch=2,
            grid=(nb,),
            in_specs=[pl.BlockSpec((tb, d), lambda b, be, nv: (last(b, nv), 0)),
                      pl.BlockSpec((1, d, de), lambda b, be, nv: (be[last(b, nv)], 0, 0)),
                      pl.BlockSpec((1, d, de), lambda b, be, nv: (be[last(b, nv)], 0, 0)),
                      pl.BlockSpec((1, de, d), lambda b, be, nv: (be[last(b, nv)], 0, 0))],
            out_specs=pl.BlockSpec((tb, d), lambda b, be, nv: (b, 0)),
        ),
        out_shape=jax.ShapeDtypeStruct((p, d), F32),
        compiler_params=_cparams(("arbitrary",)),
        name="moe_experts",
    )(block_e, n_valid, xs, w1, w3, w2)


def _combine_ln_kernel(dest_ref, gate_ref, x_ref, g_ref, b_ref, y_ref, o_ref, ob_ref, buf, sem, *, alpha):
    tm = x_ref.shape[0]

    def start(r, _):
        for k in range(TOP_K):
            _row_copy(y_ref, dest_ref[0, 0, TOP_K * r + k], buf.at[k], r, sem).start()
        return 0

    def wait(r, _):
        for k in range(TOP_K):
            _row_copy(y_ref, dest_ref[0, 0, TOP_K * r + k], buf.at[k], r, sem).wait()
        return 0

    lax.fori_loop(0, tm, start, 0, unroll=DMA_UNROLL)
    lax.fori_loop(0, tm, wait, 0, unroll=DMA_UNROLL)
    gate = gate_ref[...]
    ff = buf[0] * gate[:, 0:1] + buf[1] * gate[:, 1:2]
    x2 = _layer_norm(alpha * x_ref[...] + ff, g_ref[...], b_ref[...])
    o_ref[...] = x2
    ob_ref[...] = x2.astype(ob_ref.dtype)


def _combine_ln(dest2d, gate, x, g, b, y, alpha):
    t, d = x.shape
    tm = ROW_TM
    row = lambda i: (i, 0)
    full = lambda i: (0, 0)
    return pl.pallas_call(
        functools.partial(_combine_ln_kernel, alpha=alpha),
        grid=(t // tm,),
        in_specs=[pl.BlockSpec((1, 1, TOP_K * tm), lambda i: (i, 0, 0), memory_space=pltpu.SMEM),
                  pl.BlockSpec((tm, LANES), row), pl.BlockSpec((tm, d), row),
                  pl.BlockSpec((1, d), full), pl.BlockSpec((1, d), full),
                  pl.BlockSpec(memory_space=pl.ANY)],
        out_specs=[pl.BlockSpec((tm, d), row), pl.BlockSpec((tm, d), row)],
        out_shape=[jax.ShapeDtypeStruct((t, d), F32), jax.ShapeDtypeStruct((t, d), BF16)],
        scratch_shapes=[pltpu.VMEM((TOP_K, tm, d), F32), pltpu.SemaphoreType.DMA(())],
        compiler_params=_cparams(("arbitrary",)),
        name="moe_combine_ln",
    )(dest2d, gate, x, g, b, y)


def _moe_plan(eid, tb):
    t = eid.shape[0]
    a = t * TOP_K
    flat = eid.reshape(a)
    onehot = (flat[:, None] == jnp.arange(N_EXPERTS, dtype=jnp.int32)[None, :]).astype(jnp.int32)
    csum = jnp.cumsum(onehot, axis=0)
    rank = jnp.sum(csum * onehot, axis=1) - 1
    counts = csum[-1]
    padded = (counts + tb - 1) // tb * tb
    pend = jnp.cumsum(padded)
    pstart = pend - padded
    dest = (jnp.sum(pstart[None, :] * onehot, axis=1) + rank).astype(jnp.int32)
    nb = -(-a // tb) + N_EXPERTS
    block_e = jnp.minimum(jnp.searchsorted(pend, jnp.arange(nb, dtype=jnp.int32) * tb, side="right"),
                          N_EXPERTS - 1).astype(jnp.int32)
    n_valid = (pend[-1] // tb).astype(jnp.int32).reshape(1)
    return dest, block_e, n_valid, nb


def _segments(groups):
    segs, row0 = [], 0
    for b, s in groups:
        segs.append((row0, b, s))
        row0 += b * s
    return segs


def kernel(x_prompt, x_sample, ev_w_in, ev_w_out, diff_lambda, diff_subln, na_rpb, od_w_in, od_w_out,
           mla_q_norm, mla_w_q_up, mla_kv_norm, mla_w_kv_up, gqa_q_norm, gqa_k_norm,
           ln1_g, ln1_b, ln2_g, ln2_b, moe_w_group, moe_b_group, moe_w_expert, moe_b_expert,
           moe_w1, moe_w3, moe_w2):
    d = x_prompt.shape[-1]
    depth = ln1_g.shape[0]
    alpha = (2 * depth) ** 0.25
    groups = [(x_prompt.shape[0], x_prompt.shape[1]), (x_sample.shape[0], x_sample.shape[1])]
    segs = _segments(groups)
    x = jnp.concatenate([x_prompt.reshape(-1, d), x_sample.reshape(-1, d)], axis=0)
    t = x.shape[0]
    xb = x.astype(BF16)
    pos = jnp.concatenate([jnp.tile(jnp.arange(s, dtype=jnp.int32), b) for b, s in groups])
    posf = pos.astype(F32)

    for l in range(depth):
        i = l // 2
        if l % 2 == 0:
            tab = _diff_tables(posf)
            qscale = DIFF_QK_DIM ** -0.5 * LOG2E
            tabs = jnp.stack([tab * qscale, tab])
            h = _even_proj(xb, ev_w_in[i].astype(BF16), tabs)
            lambda_init = 0.8 - 0.6 * math.exp(-0.3 * l)
            bias_tab = _na_bias_table(na_rpb[i])
            o_a = jnp.concatenate(
                [_diff_attention(h, diff_lambda[i], diff_subln[i], r0, b, s, lambda_init)
                 for r0, b, s in segs], axis=0)
            o_b = jnp.concatenate(
                [_na_attention(h, bias_tab, 3, r0, b, s) for r0, b, s in segs], axis=0)
            w_out = ev_w_out[i]
        else:
            s1 = MLA_Q_RANK
            s2 = s1 + MLA_KV_RANK
            s3 = s2 + MLA_ROPE
            w_in = od_w_in[i]
            w_perm = jnp.concatenate(
                [w_in[:, :s2], w_in[:, s3:], w_in[:, s2:s3], jnp.zeros((d, LANES - MLA_ROPE), w_in.dtype)],
                axis=1).astype(BF16)
            c_gq = s2
            c_gk = c_gq + GQA_HEADS * HEAD_DIM
            c_gv = c_gk + GQA_KV_HEADS * HEAD_DIM
            c_kpe = c_gv + GQA_KV_HEADS * HEAD_DIM
            h = _plain_proj(xb, w_perm)
            wq = mla_w_q_up[i].reshape(MLA_Q_RANK, MLA_HEADS, MLA_NOPE + MLA_ROPE)
            wq = jnp.pad(wq, ((0, 0), (0, 0), (0, 2 * LANES - MLA_NOPE - MLA_ROPE)))
            wq = wq.reshape(MLA_Q_RANK, MLA_HEADS * 2 * LANES).astype(BF16)
            wkv = mla_w_kv_up[i].reshape(MLA_KV_RANK, MLA_HEADS, MLA_NOPE + MLA_V)
            wk = wkv[:, :, :MLA_NOPE].reshape(MLA_KV_RANK, MLA_HEADS * MLA_NOPE).astype(BF16)
            wv = wkv[:, :, MLA_NOPE:].reshape(MLA_KV_RANK, MLA_HEADS * MLA_V).astype(BF16)
            qm, km, vm, qg, kg = _odd_prep(
                h, wq, wk, wv, mla_q_norm[i].reshape(1, -1), mla_kv_norm[i].reshape(1, -1),
                gqa_q_norm[i].reshape(1, -1), gqa_k_norm[i].reshape(1, -1),
                _mla_tables(posf), _axial_tables(pos), (0, s1, c_gq, c_gk, c_kpe))
            o_a = jnp.concatenate(
                [_mla_attention(qm, km, vm, r0, b, s) for r0, b, s in segs], axis=0)
            o_b = jnp.concatenate(
                [_gqa_attention(qg, kg, h, c_gv // HEAD_DIM, r0, b, s) for r0, b, s in segs], axis=0)
            w_out = od_w_out[i]

        wr = jnp.concatenate([moe_w_group[l], moe_w_expert[l]], axis=1)
        wr = jnp.pad(wr, ((0, 0), (0, LANES - wr.shape[1])))
        wr_hi = wr.astype(BF16)
        wr = jnp.concatenate([wr_hi, (wr - wr_hi.astype(F32)).astype(BF16)], axis=1)
        br = jnp.concatenate([moe_b_group[l], moe_b_expert[l]])
        br = jnp.pad(br, (0, LANES - br.shape[0])).reshape(1, LANES)
        x1, eid, gate = _out_ln_route(o_a, o_b, w_out.astype(BF16), x, ln1_g[l].reshape(1, d),
                                      ln1_b[l].reshape(1, d), wr, br, alpha)

        dest, block_e, n_valid, nb = _moe_plan(eid[:, :TOP_K], MOE_TB)
        dest2d = dest.reshape(t // ROW_TM, 1, TOP_K * ROW_TM)
        xs = _dispatch(dest2d, x1, jnp.zeros((nb * MOE_TB, d), F32))
        y = _experts(block_e, n_valid, xs, moe_w1[l].astype(BF16), moe_w3[l].astype(BF16),
                     moe_w2[l].astype(BF16))
        x, xb = _combine_ln(dest2d, gate, x1, ln2_g[l].reshape(1, d), ln2_b[l].reshape(1, d), y, alpha)

    n_p = groups[0][0] * groups[0][1]
    return (x[:n_p].reshape(x_prompt.shape), x[n_p:].reshape(x_sample.shape))
```

```python
import functools
import math

import jax
import jax.numpy as jnp
from jax import lax
from jax.experimental import pallas as pl
from jax.experimental.pallas import tpu as pltpu

F32 = jnp.float32
BF16 = jnp.bfloat16

GRID_W = 64
HEAD_DIM = 128
ROPE_THETA = 500000.0
DIFF_HEADS = 8
DIFF_QK_DIM = HEAD_DIM // 2
DIFF_ROT = DIFF_QK_DIM // 4
NA_HEADS = 8
NA_KH = 8
NA_KW = 16
MLA_HEADS = 8
MLA_Q_RANK = 512
MLA_KV_RANK = 256
MLA_NOPE = 128
MLA_ROPE = 64
MLA_V = 128
MLA_THETA = 10000.0
GQA_HEADS = 8
GQA_KV_HEADS = 2
AXIAL_THETA = 10000.0
N_GROUPS = 4
EXPERTS_PER_GROUP = 8
N_EXPERTS = N_GROUPS * EXPERTS_PER_GROUP
TOP_K = 2
LN_EPS = 1e-5
RMS_EPS = 1e-6
SUBLN_EPS = 1e-5

LANES = 128
V7X_VMEM_BYTES = 64 * 1024 * 1024
VMEM_LIMIT = V7X_VMEM_BYTES * 7 // 8

LOG2E = math.log2(math.e)
NEG_BIG = -1e30

PROJ_TM = 512
PROJ_TN = 1024
ROW_TM = 256
FLASH_ROWS = 512
FLASH_TK = 1024
FLASH_MAX_TILES = 16
NA_ROWS = 8
MOE_TB = 256
DMA_UNROLL = 8


def _cparams(sem):
    return pltpu.CompilerParams(dimension_semantics=sem, vmem_limit_bytes=VMEM_LIMIT)


def _rope_tables(pos, lane_freq, lane_kind):
    if pos.ndim == 1:
        pos = pos[:, None]
    ang = pos * lane_freq[None, :]
    cos, sin = jnp.cos(ang), jnp.sin(ang)
    kind = lane_kind[None, :]
    c = jnp.where(kind == 2, 1.0, cos)
    s1 = jnp.where(kind == 0, -sin, 0.0)
    s2 = jnp.where(kind == 1, sin, 0.0)
    return jnp.stack([c, s1, s2]).astype(F32)


def _inv_freq(dim, theta):
    return theta ** (-jnp.arange(0, dim, 2, dtype=F32) / dim)


def _diff_tables(pos):
    lane = jnp.arange(LANES)
    within = lane % DIFF_QK_DIM
    half = DIFF_ROT // 2
    inv = _inv_freq(DIFF_ROT, ROPE_THETA)
    freq = inv[within % half]
    kind = jnp.where(within < half, 0, jnp.where(within < DIFF_ROT, 1, 2))
    return _rope_tables(pos, freq, kind)


def _mla_tables(pos):
    lane = jnp.arange(LANES)
    half = MLA_ROPE // 2
    inv = _inv_freq(MLA_ROPE, MLA_THETA)
    freq = inv[lane % half]
    kind = jnp.where(lane < half, 0, jnp.where(lane < MLA_ROPE, 1, 2))
    return _rope_tables(pos, freq, kind)


def _axial_tables(pos):
    lane = jnp.arange(LANES)
    hd = HEAD_DIM // 2
    half = hd // 2
    inv = _inv_freq(hd, AXIAL_THETA)
    freq = inv[lane % half]
    kind = jnp.where((lane % hd) < half, 0, 1)
    p = jnp.where(lane[None, :] < hd, (pos // GRID_W)[:, None], (pos % GRID_W)[:, None])
    return _rope_tables(p.astype(F32), freq, kind)


def _apply_rope(x, c, s1, s2, half):
    return x * c + pltpu.roll(x, LANES - half, 1) * s1 + pltpu.roll(x, half, 1) * s2


def _even_proj_kernel(x_ref, w_ref, tab_ref, o_ref, *, half):
    j = pl.program_id(0)
    acc = jnp.dot(x_ref[...], w_ref[...], preferred_element_type=F32)

    @pl.when(j < 2)
    def _():
        c, s1, s2 = tab_ref[0, 0], tab_ref[0, 1], tab_ref[0, 2]
        for ci in range(acc.shape[1] // LANES):
            sl = slice(ci * LANES, (ci + 1) * LANES)
            o_ref[:, sl] = _apply_rope(acc[:, sl], c, s1, s2, half).astype(o_ref.dtype)

    @pl.when(j >= 2)
    def _():
        o_ref[...] = acc.astype(o_ref.dtype)


def _even_proj(xb, w, tabs):
    t, d = xb.shape
    n = w.shape[1]
    tm, tn = PROJ_TM, PROJ_TN
    assert t % tm == 0 and n % tn == 0 and DIFF_HEADS * HEAD_DIM == tn
    return pl.pallas_call(
        functools.partial(_even_proj_kernel, half=DIFF_ROT // 2),
        grid=(n // tn, t // tm),
        in_specs=[
            pl.BlockSpec((tm, d), lambda j, i: (i, 0)),
            pl.BlockSpec((d, tn), lambda j, i: (0, j)),
            pl.BlockSpec((1, 3, tm, LANES), lambda j, i: (jnp.minimum(j, 1), 0, i, 0)),
        ],
        out_specs=pl.BlockSpec((tm, tn), lambda j, i: (i, j)),
        out_shape=jax.ShapeDtypeStruct((t, n), BF16),
        compiler_params=_cparams(("arbitrary", "arbitrary")),
        name="even_proj",
    )(xb, w, tabs)


def _plain_proj_kernel(x_ref, w_ref, o_ref):
    o_ref[...] = jnp.dot(x_ref[...], w_ref[...], preferred_element_type=F32).astype(o_ref.dtype)


def _plain_proj(xb, w):
    t, d = xb.shape
    n = w.shape[1]
    tm = PROJ_TM
    assert t % tm == 0 and n % LANES == 0
    return pl.pallas_call(
        _plain_proj_kernel,
        grid=(t // tm,),
        in_specs=[pl.BlockSpec((tm, d), lambda i: (i, 0)),
                  pl.BlockSpec((d, n), lambda i: (0, 0))],
        out_specs=pl.BlockSpec((tm, n), lambda i: (i, 0)),
        out_shape=jax.ShapeDtypeStruct((t, n), BF16),
        compiler_params=_cparams(("arbitrary",)),
        name="odd_proj",
    )(xb, w)


def _flash_scratch(rows, dqk, dv, tk, seq):
    return [pltpu.VMEM((rows, dqk), BF16),
            pltpu.VMEM((seq, dv + LANES), BF16),
            pltpu.VMEM((2, rows, tk), F32),
            pltpu.VMEM((rows, LANES), F32),
            pltpu.VMEM((rows, dv + LANES), F32)]


def _flash_loop(first, q_ref, k_ref, v_ref, vaug_ref, s_ref, m_ref, acc_ref, tk):
    n = k_ref.shape[0] // tk
    assert n <= FLASH_MAX_TILES
    nc = tk // LANES
    dv = v_ref.shape[1]

    @pl.when(first)
    def _():
        lane = lax.broadcasted_iota(jnp.int32, (v_ref.shape[0], LANES), 1)
        vaug_ref[:, 0:dv] = v_ref[...]
        vaug_ref[:, dv:] = jnp.where(lane == 0, 1.0, 0.0).astype(BF16)

    def scores(j):
        s_ref[j % 2] = lax.dot_general(q_ref[...], k_ref[j * tk:(j + 1) * tk, :],
                                       (((1,), (1,)), ((), ())), preferred_element_type=F32)

    def softmax_pv(j):
        slot = j % 2
        cmax = s_ref[slot, :, 0:LANES]
        for c in range(1, nc):
            cmax = jnp.maximum(cmax, s_ref[slot, :, c * LANES:(c + 1) * LANES])
        m_prev = m_ref[...]
        m_new = jnp.maximum(m_prev, jnp.max(cmax, axis=1, keepdims=True))
        a = jnp.exp2(m_prev - m_new)
        p = jnp.concatenate(
            [jnp.exp2(s_ref[slot, :, c * LANES:(c + 1) * LANES] - m_new).astype(BF16)
             for c in range(nc)], axis=1)
        pv = jnp.dot(p, vaug_ref[j * tk:(j + 1) * tk, :], preferred_element_type=F32)
        acc_ref[...] = jnp.concatenate([a, a], axis=1) * acc_ref[...] + pv
        m_ref[...] = m_new

    m_ref[...] = jnp.full(m_ref.shape, -jnp.inf, F32)
    acc_ref[...] = jnp.zeros(acc_ref.shape, F32)
    scores(0)
    for j in range(n):
        if j + 1 < n:
            scores(j + 1)
        softmax_pv(j)
    return acc_ref[:, 0:dv], acc_ref[:, dv:dv + 1]


def _flash_tk(seq):
    return min(FLASH_TK, seq // 2)


def _diff_attn_kernel(lam_ref, g_ref, q_ref, k_ref, v_ref, o_ref, q_scr, *scr, tk, lambda_init):
    tq = q_ref.shape[0]
    q = q_ref[...]
    lane = lax.broadcasted_iota(jnp.int32, q.shape, 1)
    zero = jnp.zeros_like(q)
    q_scr[0:tq] = jnp.where(lane < DIFF_QK_DIM, q, zero)
    q_scr[tq:2 * tq] = jnp.where(lane >= DIFF_QK_DIM, q, zero)
    acc, l = _flash_loop(pl.program_id(2) == 0, q_scr, k_ref, v_ref, *scr, tk)
    o = acc / l
    lv = lam_ref[...].astype(F32)
    lam = (jnp.exp(jnp.sum(lv[0:1] * lv[1:2], axis=1, keepdims=True))
           - jnp.exp(jnp.sum(lv[2:3] * lv[3:4], axis=1, keepdims=True)) + lambda_init)
    o = o[:tq] - lam * o[tq:]
    y = o * lax.rsqrt(jnp.mean(o * o, axis=1, keepdims=True) + SUBLN_EPS)
    o_ref[...] = (y * g_ref[...] * (1.0 - lambda_init)).astype(o_ref.dtype)


def _diff_attention(h, lam_vec, subln_g, row0, batch, seq, lambda_init):
    tq = FLASH_ROWS // 2
    tk = _flash_tk(seq)
    assert seq % tq == 0 and seq % tk == 0 and row0 % seq == 0
    nq = seq // tq
    hcols = DIFF_HEADS
    return pl.pallas_call(
        functools.partial(_diff_attn_kernel, tk=tk, lambda_init=lambda_init),
        grid=(batch, DIFF_HEADS, nq),
        in_specs=[
            pl.BlockSpec((4, DIFF_QK_DIM), lambda b, hh, i: (0, 0)),
            pl.BlockSpec((1, HEAD_DIM), lambda b, hh, i: (0, 0)),
            pl.BlockSpec((tq, HEAD_DIM), lambda b, hh, i: (row0 // tq + b * nq + i, hh)),
            pl.BlockSpec((seq, HEAD_DIM), lambda b, hh, i: (row0 // seq + b, hcols + hh)),
            pl.BlockSpec((seq, HEAD_DIM), lambda b, hh, i: (row0 // seq + b, 2 * hcols + hh)),
        ],
        out_specs=pl.BlockSpec((tq, HEAD_DIM), lambda b, hh, i: (b * nq + i, hh)),
        out_shape=jax.ShapeDtypeStruct((batch * seq, DIFF_HEADS * HEAD_DIM), BF16),
        scratch_shapes=_flash_scratch(2 * tq, HEAD_DIM, HEAD_DIM, tk, seq),
        compiler_params=_cparams(("arbitrary",) * 3),
        name="diff_attn",
    )(lam_vec, subln_g.reshape(1, HEAD_DIM), h, h, h)


def _mla_attn_kernel(q_ref, k_ref, v_ref, o_ref, q_scr, *scr, tk):
    del q_scr
    acc, l = _flash_loop(pl.program_id(2) == 0, q_ref, k_ref, v_ref, *scr, tk)
    o_ref[...] = (acc / l).astype(o_ref.dtype)


def _mla_attention(qm, km, vm, row0, batch, seq):
    tq = FLASH_ROWS
    tk = _flash_tk(seq)
    assert seq % tq == 0 and seq % tk == 0 and row0 % seq == 0
    nq = seq // tq
    dqk = 2 * LANES
    return pl.pallas_call(
        functools.partial(_mla_attn_kernel, tk=tk),
        grid=(batch, MLA_HEADS, nq),
        in_specs=[
            pl.BlockSpec((tq, dqk), lambda b, hh, i: (row0 // tq + b * nq + i, hh)),
            pl.BlockSpec((seq, dqk), lambda b, hh, i: (row0 // seq + b, hh)),
            pl.BlockSpec((seq, MLA_V), lambda b, hh, i: (row0 // seq + b, hh)),
        ],
        out_specs=pl.BlockSpec((tq, MLA_V), lambda b, hh, i: (b * nq + i, hh)),
        out_shape=jax.ShapeDtypeStruct((batch * seq, MLA_HEADS * MLA_V), BF16),
        scratch_shapes=_flash_scratch(tq, 8, MLA_V, tk, seq),
        compiler_params=_cparams(("arbitrary",) * 3),
        name="mla_attn",
    )(qm, km, vm)


def _gqa_attn_kernel(q_ref, k_ref, v_ref, o_ref, q_scr, *scr, tk, group):
    tq = q_ref.shape[0]
    for g in range(group):
        q_scr[g * tq:(g + 1) * tq] = q_ref[:, g * HEAD_DIM:(g + 1) * HEAD_DIM]
    acc, l = _flash_loop(pl.program_id(2) == 0, q_scr, k_ref, v_ref, *scr, tk)
    o = acc / l
    for g in range(group):
        o_ref[:, g * HEAD_DIM:(g + 1) * HEAD_DIM] = o[g * tq:(g + 1) * tq].astype(o_ref.dtype)


def _gqa_attention(qg, kg, h, v_col0, row0, batch, seq):
    group = GQA_HEADS // GQA_KV_HEADS
    tq = FLASH_ROWS // group
    tk = _flash_tk(seq)
    assert seq % tq == 0 and seq % tk == 0 and row0 % seq == 0
    nq = seq // tq
    return pl.pallas_call(
        functools.partial(_gqa_attn_kernel, tk=tk, group=group),
        grid=(batch, GQA_KV_HEADS, nq),
        in_specs=[
            pl.BlockSpec((tq, group * HEAD_DIM), lambda b, n, i: (row0 // tq + b * nq + i, n)),
            pl.BlockSpec((seq, HEAD_DIM), lambda b, n, i: (row0 // seq + b, n)),
            pl.BlockSpec((seq, HEAD_DIM), lambda b, n, i: (row0 // seq + b, v_col0 + n)),
        ],
        out_specs=pl.BlockSpec((tq, group * HEAD_DIM), lambda b, n, i: (b * nq + i, n)),
        out_shape=jax.ShapeDtypeStruct((batch * seq, GQA_HEADS * HEAD_DIM), BF16),
        scratch_shapes=_flash_scratch(group * tq, HEAD_DIM, HEAD_DIM, tk, seq),
        compiler_params=_cparams(("arbitrary",) * 3),
        name="gqa_attn",
    )(qg, kg, h)


def _na_bias_table(rpb):
    cols = jnp.arange(GRID_W)
    start = jnp.clip(cols - NA_KW // 2, 0, GRID_W - NA_KW)
    kc = cols[None, :]
    valid = (kc >= start[:, None]) & (kc < start[:, None] + NA_KW)
    dc = jnp.clip(kc - cols[:, None] + (NA_KW - 1), 0, 2 * NA_KW - 2)
    tab = rpb.astype(F32)[:, :, dc]
    tab = jnp.where(valid[None, None], tab, NEG_BIG)
    return jnp.concatenate([tab[:, :-1], tab[:, 1:]], axis=-1)


def _na_kernel(q_ref, kp_ref, kc_ref, kn_ref, vp_ref, vc_ref, vn_ref, bias_ref, o_ref,
               kbuf, vbuf, *, rows, scale):
    i = pl.program_id(1)
    blk = NA_ROWS * GRID_W
    kbuf[0:blk] = kp_ref[...]
    kbuf[blk:2 * blk] = kc_ref[...]
    kbuf[2 * blk:3 * blk] = kn_ref[...]
    vbuf[0:blk] = vp_ref[...]
    vbuf[blk:2 * blk] = vc_ref[...]
    vbuf[2 * blk:3 * blk] = vn_ref[...]
    win = NA_KH * GRID_W

    def row_body(j, _):
        r = i * NA_ROWS + j
        rs = jnp.clip(r - NA_KH // 2, 0, rows - NA_KH)
        dr0 = rs - r + (NA_KH - 1)
        koff = pl.multiple_of((rs - (i - 1) * NA_ROWS) * GRID_W, GRID_W)
        qoff = pl.multiple_of(j * GRID_W, GRID_W)
        for hh in range(NA_HEADS):
            cs = slice(hh * HEAD_DIM, (hh + 1) * HEAD_DIM)
            q = (q_ref[pl.ds(qoff, GRID_W), cs].astype(F32) * scale).astype(BF16)
            kw = kbuf[pl.ds(koff, win), cs]
            vw = vbuf[pl.ds(koff, win), cs]
            s = lax.dot_general(q, kw, (((1,), (1,)), ((), ())), preferred_element_type=F32)
            s = s + jnp.concatenate(
                [bias_ref[hh, dr0 + 2 * c] for c in range(win // LANES)], axis=1) * LOG2E
            m = jnp.max(s, axis=1, keepdims=True)
            p = jnp.exp2(s - m)
            l = jnp.sum(p, axis=1, keepdims=True)
            o = jnp.dot(p.astype(BF16), vw, preferred_element_type=F32) / l
            o_ref[pl.ds(qoff, GRID_W), cs] = o.astype(o_ref.dtype)
        return 0

    lax.fori_loop(0, NA_ROWS, row_body, 0)


def _na_attention(h, bias_tab, col0, row0, batch, seq):
    rows = seq // GRID_W
    assert seq % GRID_W == 0 and rows % NA_ROWS == 0 and rows >= NA_KH
    nb = rows // NA_ROWS
    blk = NA_ROWS * GRID_W
    width = NA_HEADS * HEAD_DIM
    assert row0 % blk == 0
    base = row0 // blk

    def cur(c):
        return lambda b, i: (base + b * nb + i, c)

    def prev(c):
        return lambda b, i: (base + b * nb + jnp.maximum(i - 1, 0), c)

    def nxt(c):
        return lambda b, i: (base + b * nb + jnp.minimum(i + 1, nb - 1), c)

    bspec = lambda f: pl.BlockSpec((blk, width), f)
    return pl.pallas_call(
        functools.partial(_na_kernel, rows=rows, scale=HEAD_DIM ** -0.5 * LOG2E),
        grid=(batch, nb),
        in_specs=[bspec(cur(col0)),
                  bspec(prev(col0 + 1)), bspec(cur(col0 + 1)), bspec(nxt(col0 + 1)),
                  bspec(prev(col0 + 2)), bspec(cur(col0 + 2)), bspec(nxt(col0 + 2)),
                  pl.BlockSpec(bias_tab.shape, lambda b, i: (0, 0, 0, 0))],
        out_specs=pl.BlockSpec((blk, width), lambda b, i: (b * nb + i, 0)),
        out_shape=jax.ShapeDtypeStruct((batch * seq, width), BF16),
        scratch_shapes=[pltpu.VMEM((3 * blk, width), BF16), pltpu.VMEM((3 * blk, width), BF16)],
        compiler_params=_cparams(("arbitrary",) * 2),
        name="na_attn",
    )(h, h, h, h, h, h, h, bias_tab)


def _rms(x, g, eps):
    return x * lax.rsqrt(jnp.mean(x * x, axis=1, keepdims=True) + eps) * g


def _odd_prep_kernel(h_ref, wq_ref, wk_ref, wv_ref, qn_ref, kvn_ref, gqn_ref, gkn_ref,
                     mtab_ref, atab_ref, qm_ref, km_ref, vm_ref, qg_ref, kg_ref, *, cols):
    c_qc, c_kvc, c_gq, c_gk, c_kpe = cols
    mla_scale = (MLA_NOPE + MLA_ROPE) ** -0.5 * LOG2E
    gqa_scale = HEAD_DIM ** -0.5 * LOG2E
    mc, ms1, ms2 = mtab_ref[0], mtab_ref[1], mtab_ref[2]
    ac, as1, as2 = atab_ref[0], atab_ref[1], atab_ref[2]
    mhalf = MLA_ROPE // 2
    ahalf = HEAD_DIM // 4

    qc = _rms(h_ref[:, c_qc:c_qc + MLA_Q_RANK].astype(F32), qn_ref[...], RMS_EPS)
    q = jnp.dot(qc.astype(BF16), wq_ref[...], preferred_element_type=F32)
    for hh in range(MLA_HEADS):
        lo = hh * 2 * LANES
        qm_ref[:, lo:lo + LANES] = (q[:, lo:lo + LANES] * mla_scale).astype(qm_ref.dtype)
        qr = _apply_rope(q[:, lo + LANES:lo + 2 * LANES], mc, ms1, ms2, mhalf)
        qm_ref[:, lo + LANES:lo + 2 * LANES] = (qr * mla_scale).astype(qm_ref.dtype)

    kvc = _rms(h_ref[:, c_kvc:c_kvc + MLA_KV_RANK].astype(F32), kvn_ref[...], RMS_EPS).astype(BF16)
    kn = jnp.dot(kvc, wk_ref[...], preferred_element_type=F32)
    vm_ref[...] = jnp.dot(kvc, wv_ref[...], preferred_element_type=F32).astype(vm_ref.dtype)
    kpe = _apply_rope(h_ref[:, c_kpe:c_kpe + LANES].astype(F32), mc, ms1, ms2, mhalf)
    kpe = kpe.astype(km_ref.dtype)
    for hh in range(MLA_HEADS):
        lo = hh * 2 * LANES
        km_ref[:, lo:lo + LANES] = kn[:, hh * LANES:(hh + 1) * LANES].astype(km_ref.dtype)
        km_ref[:, lo + LANES:lo + 2 * LANES] = kpe

    for hh in range(GQA_HEADS):
        sl = slice(c_gq + hh * HEAD_DIM, c_gq + (hh + 1) * HEAD_DIM)
        x = _rms(h_ref[:, sl].astype(F32), gqn_ref[...], RMS_EPS)
        x = _apply_rope(x, ac, as1, as2, ahalf) * gqa_scale
        qg_ref[:, hh * HEAD_DIM:(hh + 1) * HEAD_DIM] = x.astype(qg_ref.dtype)
    for hh in range(GQA_KV_HEADS):
        sl = slice(c_gk + hh * HEAD_DIM, c_gk + (hh + 1) * HEAD_DIM)
        x = _rms(h_ref[:, sl].astype(F32), gkn_ref[...], RMS_EPS)
        x = _apply_rope(x, ac, as1, as2, ahalf)
        kg_ref[:, hh * HEAD_DIM:(hh + 1) * HEAD_DIM] = x.astype(kg_ref.dtype)


def _odd_prep(h, wq, wk, wv, qn, kvn, gqn, gkn, mtab, atab, cols):
    t, n = h.shape
    tm = ROW_TM
    assert t % tm == 0
    row = lambda i: (i, 0)
    full = lambda i: (0, 0)
    tab = lambda i: (0, i, 0)
    widths = (MLA_HEADS * 2 * LANES, MLA_HEADS * 2 * LANES, MLA_HEADS * MLA_V,
              GQA_HEADS * HEAD_DIM, GQA_KV_HEADS * HEAD_DIM)
    return pl.pallas_call(
        functools.partial(_odd_prep_kernel, cols=cols),
        grid=(t // tm,),
        in_specs=[pl.BlockSpec((tm, n), row),
                  pl.BlockSpec(wq.shape, full), pl.BlockSpec(wk.shape, full), pl.BlockSpec(wv.shape, full),
                  pl.BlockSpec(qn.shape, full), pl.BlockSpec(kvn.shape, full),
                  pl.BlockSpec(gqn.shape, full), pl.BlockSpec(gkn.shape, full),
                  pl.BlockSpec((3, tm, LANES), tab), pl.BlockSpec((3, tm, LANES), tab)],
        out_specs=[pl.BlockSpec((tm, w), row) for w in widths],
        out_shape=[jax.ShapeDtypeStruct((t, w), BF16) for w in widths],
        compiler_params=_cparams(("arbitrary",)),
        name="odd_prep",
    )(h, wq, wk, wv, qn, kvn, gqn, gkn, mtab, atab)


def _layer_norm(y, g, b):
    mu = jnp.mean(y, axis=1, keepdims=True)
    yc = y - mu
    var = jnp.mean(yc * yc, axis=1, keepdims=True)
    return yc * lax.rsqrt(var + LN_EPS) * g + b


def _route(logits):
    lane = lax.broadcasted_iota(jnp.int32, logits.shape, 1)
    lanef = lane.astype(F32)
    big = float(LANES)
    gmask = lane < N_GROUPS
    gl = jnp.where(gmask, logits, -jnp.inf)
    gmax = jnp.max(gl, axis=1, keepdims=True)
    gsel = jnp.min(jnp.where(gl == gmax, lanef, big), axis=1, keepdims=True)
    gden = jnp.sum(jnp.where(gmask, jnp.exp(gl - gmax), 0.0), axis=1, keepdims=True)
    gprob = 1.0 / gden
    lo = N_GROUPS + EXPERTS_PER_GROUP * gsel
    emask = (lanef >= lo) & (lanef < lo + EXPERTS_PER_GROUP)
    el = jnp.where(emask, logits, -jnp.inf)
    v1 = jnp.max(el, axis=1, keepdims=True)
    i1 = jnp.min(jnp.where(el == v1, lanef, big), axis=1, keepdims=True)
    el2 = jnp.where(lanef == i1, -jnp.inf, el)
    v2 = jnp.max(el2, axis=1, keepdims=True)
    i2 = jnp.min(jnp.where(el2 == v2, lanef, big), axis=1, keepdims=True)
    e = jnp.exp(v2 - v1)
    g1 = gprob / (1.0 + e)
    g2 = gprob * e / (1.0 + e)
    gate = jnp.where(lane == 0, g1, jnp.where(lane == 1, g2, 0.0))
    return i1 - N_GROUPS, i2 - N_GROUPS, gate


def _rank_in_expert(e1, e2, cnt_ref):
    tm = e1.shape[0]
    lanef = lax.broadcasted_iota(jnp.int32, (tm, LANES), 1).astype(F32)
    sel1 = lanef == e1
    sel2 = lanef == e2
    both = jnp.where(sel1, 1.0, jnp.where(sel2, 1.0, 0.0))
    row = lax.broadcasted_iota(jnp.int32, (tm, tm), 0)
    col = lax.broadcasted_iota(jnp.int32, (tm, tm), 1)
    ltri = jnp.where(col < row, 1.0, 0.0).astype(BF16)
    before = jnp.dot(ltri, both.astype(BF16), preferred_element_type=F32) + cnt_ref[...]
    r1 = jnp.sum(jnp.where(sel1, before, 0.0), axis=1, keepdims=True)
    r2 = jnp.sum(jnp.where(sel2, before, 0.0), axis=1, keepdims=True)
    cnt_ref[...] = cnt_ref[...] + jnp.sum(both, axis=0, keepdims=True)
    return r1, r2


def _out_ln_route_kernel(a1_ref, a2_ref, w_ref, x_ref, g_ref, b_ref, wr_ref, br_ref,
                         x1_ref, meta_ref, gate_ref, cnt_ref, *, alpha):
    @pl.when(pl.program_id(0) == 0)
    def _():
        cnt_ref[...] = jnp.zeros(cnt_ref.shape, F32)

    k1 = a1_ref.shape[1]
    mix = (jnp.dot(a1_ref[...], w_ref[0:k1, :], preferred_element_type=F32)
           + jnp.dot(a2_ref[...], w_ref[k1:, :], preferred_element_type=F32))
    x1 = _layer_norm(alpha * x_ref[...] + mix, g_ref[...], b_ref[...])
    x1_ref[...] = x1
    x_hi = x1.astype(BF16)
    x_mid = (x1 - x_hi.astype(F32)).astype(BF16)
    t = (jnp.dot(x_hi, wr_ref[...], preferred_element_type=F32)
         + jnp.dot(x_mid, wr_ref[...], preferred_element_type=F32))
    logits = t[:, :LANES] + t[:, LANES:] + br_ref[...]
    e1, e2, gate = _route(logits)
    r1, r2 = _rank_in_expert(e1, e2, cnt_ref)
    lane = lax.broadcasted_iota(jnp.int32, gate.shape, 1)
    meta = jnp.where(lane == 0, e1, jnp.where(lane == 1, e2,
                                              jnp.where(lane == 2, r1, jnp.where(lane == 3, r2, 0.0))))
    meta_ref[...] = meta.astype(jnp.int32)
    gate_ref[...] = gate


def _out_ln_route(a1, a2, w, x, g, b, wr, br, alpha):
    t, d = x.shape
    tm = ROW_TM
    assert t % tm == 0
    row = lambda i: (i, 0)
    full = lambda i: (0, 0)
    return pl.pallas_call(
        functools.partial(_out_ln_route_kernel, alpha=alpha),
        grid=(t // tm,),
        in_specs=[pl.BlockSpec((tm, a1.shape[1]), row), pl.BlockSpec((tm, a2.shape[1]), row),
                  pl.BlockSpec(w.shape, full), pl.BlockSpec((tm, d), row),
                  pl.BlockSpec((1, d), full), pl.BlockSpec((1, d), full),
                  pl.BlockSpec(wr.shape, full), pl.BlockSpec((1, LANES), full)],
        out_specs=[pl.BlockSpec((tm, d), row), pl.BlockSpec((tm, LANES), row),
                   pl.BlockSpec((tm, LANES), row), pl.BlockSpec((1, LANES), full)],
        out_shape=[jax.ShapeDtypeStruct((t, d), F32), jax.ShapeDtypeStruct((t, LANES), jnp.int32),
                   jax.ShapeDtypeStruct((t, LANES), F32), jax.ShapeDtypeStruct((1, LANES), F32)],
        compiler_params=_cparams(("arbitrary",)),
        name="out_ln_route",
    )(a1, a2, w, x, g, b, wr, br)


def _row_copy(src_ref, src_row, dst_ref, dst_row, sem):
    return pltpu.make_async_copy(src_ref.at[pl.ds(src_row, 1)], dst_ref.at[pl.ds(dst_row, 1)], sem)


def _dispatch_kernel(dest_ref, x_ref, xs_in_ref, xs_ref, sem):
    del xs_in_ref
    tm = x_ref.shape[0]

    def start(r, _):
        for k in range(TOP_K):
            _row_copy(x_ref, r, xs_ref, dest_ref[0, 0, TOP_K * r + k], sem).start()
        return 0

    def wait(r, _):
        for k in range(TOP_K):
            _row_copy(x_ref, r, xs_ref, dest_ref[0, 0, TOP_K * r + k], sem).wait()
        return 0

    lax.fori_loop(0, tm, start, 0, unroll=DMA_UNROLL)
    lax.fori_loop(0, tm, wait, 0, unroll=DMA_UNROLL)


def _dispatch(dest2d, x, xs_init):
    t, d = x.shape
    tm = ROW_TM
    return pl.pallas_call(
        _dispatch_kernel,
        grid=(t // tm,),
        in_specs=[pl.BlockSpec((1, 1, TOP_K * tm), lambda i: (i, 0, 0), memory_space=pltpu.SMEM),
                  pl.BlockSpec((tm, d), lambda i: (i, 0)),
                  pl.BlockSpec(memory_space=pl.ANY)],
        out_specs=pl.BlockSpec(memory_space=pl.ANY),
        out_shape=jax.ShapeDtypeStruct(xs_init.shape, xs_init.dtype),
        scratch_shapes=[pltpu.SemaphoreType.DMA(())],
        input_output_aliases={2: 0},
        compiler_params=_cparams(("arbitrary",)),
        name="moe_dispatch",
    )(dest2d, x, xs_init)


def _expert_kernel(be_ref, nv_ref, xs_ref, w1_ref, w3_ref, w2_ref, y_ref, w1b, w3b, w2b):
    b = pl.program_id(0)
    valid = b < nv_ref[0]
    new_expert = jnp.logical_or(b == 0, be_ref[b] != be_ref[jnp.maximum(b - 1, 0)])

    @pl.when(jnp.logical_and(valid, new_expert))
    def _():
        w1b[...] = w1_ref[0].astype(BF16)
        w3b[...] = w3_ref[0].astype(BF16)
        w2b[...] = w2_ref[0].astype(BF16)

    @pl.when(valid)
    def _():
        x = xs_ref[...].astype(BF16)
        h1 = jnp.dot(x, w1b[...], preferred_element_type=F32)
        h3 = jnp.dot(x, w3b[...], preferred_element_type=F32)
        hm = (h1 / (1.0 + jnp.exp(-h1)) * h3).astype(BF16)
        y_ref[...] = jnp.dot(hm, w2b[...], preferred_element_type=F32).astype(y_ref.dtype)

    @pl.when(jnp.logical_not(valid))
    def _():
        y_ref[...] = jnp.zeros_like(y_ref)


def _experts(block_e, n_valid, xs, w1, w3, w2):
    p, d = xs.shape
    tb = MOE_TB
    nb = p // tb
    de = w1.shape[2]
    last = lambda b, nv: jnp.minimum(b, nv[0] - 1)
    return pl.pallas_call(
        _expert_kernel,
        grid_spec=pltpu.PrefetchScalarGridSpec(
            num_scalar_prefetch=2,
            grid=(nb,),
            in_specs=[pl.BlockSpec((tb, d), lambda b, be, nv: (last(b, nv), 0)),
                      pl.BlockSpec((1, d, de), lambda b, be, nv: (be[last(b, nv)], 0, 0)),
                      pl.BlockSpec((1, d, de), lambda b, be, nv: (be[last(b, nv)], 0, 0)),
                      pl.BlockSpec((1, de, d), lambda b, be, nv: (be[last(b, nv)], 0, 0))],
            out_specs=pl.BlockSpec((tb, d), lambda b, be, nv: (b, 0)),
            scratch_shapes=[pltpu.VMEM((d, de), BF16), pltpu.VMEM((d, de), BF16),
                            pltpu.VMEM((de, d), BF16)],
        ),
        out_shape=jax.ShapeDtypeStruct((p, d), F32),
        compiler_params=_cparams(("arbitrary",)),
        name="moe_experts",
    )(block_e, n_valid, xs, w1, w3, w2)


def _combine_ln_kernel(dest_ref, gate_ref, x_ref, g_ref, b_ref, y_ref, o_ref, *rest, alpha):
    *maybe_ob_ref, buf, sem = rest
    tm = x_ref.shape[0]

    def start(r, _):
        for k in range(TOP_K):
            _row_copy(y_ref, dest_ref[0, 0, TOP_K * r + k], buf.at[k], r, sem).start()
        return 0

    def wait(r, _):
        for k in range(TOP_K):
            _row_copy(y_ref, dest_ref[0, 0, TOP_K * r + k], buf.at[k], r, sem).wait()
        return 0

    lax.fori_loop(0, tm, start, 0, unroll=DMA_UNROLL)
    lax.fori_loop(0, tm, wait, 0, unroll=DMA_UNROLL)
    gate = gate_ref[...]
    ff = buf[0] * gate[:, 0:1] + buf[1] * gate[:, 1:2]
    x2 = _layer_norm(alpha * x_ref[...] + ff, g_ref[...], b_ref[...])
    o_ref[...] = x2
    for ob_ref in maybe_ob_ref:
        ob_ref[...] = x2.astype(ob_ref.dtype)


def _combine_ln(dest2d, gate, x, g, b, y, alpha, row0, nrows, with_bf16):
    d = x.shape[1]
    tm = ROW_TM
    assert row0 % tm == 0 and nrows % tm == 0
    base = row0 // tm
    row_in = lambda i: (base + i, 0)
    row_out = lambda i: (i, 0)
    full = lambda i: (0, 0)
    n_out = 2 if with_bf16 else 1
    return pl.pallas_call(
        functools.partial(_combine_ln_kernel, alpha=alpha),
        grid=(nrows // tm,),
        in_specs=[pl.BlockSpec((1, 1, TOP_K * tm), lambda i: (base + i, 0, 0), memory_space=pltpu.SMEM),
                  pl.BlockSpec((tm, LANES), row_in), pl.BlockSpec((tm, d), row_in),
                  pl.BlockSpec((1, d), full), pl.BlockSpec((1, d), full),
                  pl.BlockSpec(memory_space=pl.ANY)],
        out_specs=[pl.BlockSpec((tm, d), row_out)] * n_out,
        out_shape=[jax.ShapeDtypeStruct((nrows, d), F32), jax.ShapeDtypeStruct((nrows, d), BF16)][:n_out],
        scratch_shapes=[pltpu.VMEM((TOP_K, tm, d), F32), pltpu.SemaphoreType.DMA(())],
        compiler_params=_cparams(("arbitrary",)),
        name="moe_combine_ln",
    )(dest2d, gate, x, g, b, y)


def _moe_plan(eid, rank, counts, tb):
    a = eid.shape[0] * TOP_K
    padded = (counts + tb - 1) // tb * tb
    pend = jnp.cumsum(padded)
    pstart = pend - padded
    dest = (jnp.take(pstart, eid) + rank).astype(jnp.int32)
    nb = -(-a // tb) + N_EXPERTS
    block_e = jnp.minimum(jnp.searchsorted(pend, jnp.arange(nb, dtype=jnp.int32) * tb, side="right"),
                          N_EXPERTS - 1).astype(jnp.int32)
    n_valid = (pend[-1] // tb).astype(jnp.int32).reshape(1)
    return dest, block_e, n_valid, nb


def _segments(groups):
    segs, row0 = [], 0
    for b, s in groups:
        segs.append((row0, b, s))
        row0 += b * s
    return segs


def kernel(x_prompt, x_sample, ev_w_in, ev_w_out, diff_lambda, diff_subln, na_rpb, od_w_in, od_w_out,
           mla_q_norm, mla_w_q_up, mla_kv_norm, mla_w_kv_up, gqa_q_norm, gqa_k_norm,
           ln1_g, ln1_b, ln2_g, ln2_b, moe_w_group, moe_b_group, moe_w_expert, moe_b_expert,
           moe_w1, moe_w3, moe_w2):
    d = x_prompt.shape[-1]
    depth = ln1_g.shape[0]
    alpha = (2 * depth) ** 0.25
    groups = [(x_prompt.shape[0], x_prompt.shape[1]), (x_sample.shape[0], x_sample.shape[1])]
    segs = _segments(groups)
    x = jnp.concatenate([x_prompt.reshape(-1, d), x_sample.reshape(-1, d)], axis=0)
    t = x.shape[0]
    xb = x.astype(BF16)
    pos = jnp.concatenate([jnp.tile(jnp.arange(s, dtype=jnp.int32), b) for b, s in groups])
    posf = pos.astype(F32)

    for l in range(depth):
        i = l // 2
        if l % 2 == 0:
            tab = _diff_tables(posf)
            qscale = DIFF_QK_DIM ** -0.5 * LOG2E
            tabs = jnp.stack([tab * qscale, tab])
            h = _even_proj(xb, ev_w_in[i].astype(BF16), tabs)
            lambda_init = 0.8 - 0.6 * math.exp(-0.3 * l)
            bias_tab = _na_bias_table(na_rpb[i])
            o_a = jnp.concatenate(
                [_diff_attention(h, diff_lambda[i], diff_subln[i], r0, b, s, lambda_init)
                 for r0, b, s in segs], axis=0)
            o_b = jnp.concatenate(
                [_na_attention(h, bias_tab, 3, r0, b, s) for r0, b, s in segs], axis=0)
            w_out = ev_w_out[i]
        else:
            s1 = MLA_Q_RANK
            s2 = s1 + MLA_KV_RANK
            s3 = s2 + MLA_ROPE
            w_in = od_w_in[i]
            w_perm = jnp.concatenate(
                [w_in[:, :s2], w_in[:, s3:], w_in[:, s2:s3], jnp.zeros((d, LANES - MLA_ROPE), w_in.dtype)],
                axis=1).astype(BF16)
            c_gq = s2
            c_gk = c_gq + GQA_HEADS * HEAD_DIM
            c_gv = c_gk + GQA_KV_HEADS * HEAD_DIM
            c_kpe = c_gv + GQA_KV_HEADS * HEAD_DIM
            h = _plain_proj(xb, w_perm)
            wq = mla_w_q_up[i].reshape(MLA_Q_RANK, MLA_HEADS, MLA_NOPE + MLA_ROPE)
            wq = jnp.pad(wq, ((0, 0), (0, 0), (0, 2 * LANES - MLA_NOPE - MLA_ROPE)))
            wq = wq.reshape(MLA_Q_RANK, MLA_HEADS * 2 * LANES).astype(BF16)
            wkv = mla_w_kv_up[i].reshape(MLA_KV_RANK, MLA_HEADS, MLA_NOPE + MLA_V)
            wk = wkv[:, :, :MLA_NOPE].reshape(MLA_KV_RANK, MLA_HEADS * MLA_NOPE).astype(BF16)
            wv = wkv[:, :, MLA_NOPE:].reshape(MLA_KV_RANK, MLA_HEADS * MLA_V).astype(BF16)
            qm, km, vm, qg, kg = _odd_prep(
                h, wq, wk, wv, mla_q_norm[i].reshape(1, -1), mla_kv_norm[i].reshape(1, -1),
                gqa_q_norm[i].reshape(1, -1), gqa_k_norm[i].reshape(1, -1),
                _mla_tables(posf), _axial_tables(pos), (0, s1, c_gq, c_gk, c_kpe))
            o_a = jnp.concatenate(
                [_mla_attention(qm, km, vm, r0, b, s) for r0, b, s in segs], axis=0)
            o_b = jnp.concatenate(
                [_gqa_attention(qg, kg, h, c_gv // HEAD_DIM, r0, b, s) for r0, b, s in segs], axis=0)
            w_out = od_w_out[i]

        wr = jnp.concatenate([moe_w_group[l], moe_w_expert[l]], axis=1)
        wr = jnp.pad(wr, ((0, 0), (0, LANES - wr.shape[1])))
        wr_hi = wr.astype(BF16)
        wr = jnp.concatenate([wr_hi, (wr - wr_hi.astype(F32)).astype(BF16)], axis=1)
        br = jnp.concatenate([moe_b_group[l], moe_b_expert[l]])
        br = jnp.pad(br, (0, LANES - br.shape[0])).reshape(1, LANES)
        x1, meta, gate, cnt = _out_ln_route(o_a, o_b, w_out.astype(BF16), x, ln1_g[l].reshape(1, d),
                                            ln1_b[l].reshape(1, d), wr, br, alpha)

        dest, block_e, n_valid, nb = _moe_plan(meta[:, :TOP_K], meta[:, TOP_K:2 * TOP_K],
                                               cnt[0, :N_EXPERTS].astype(jnp.int32), MOE_TB)
        dest2d = dest.reshape(t // ROW_TM, 1, TOP_K * ROW_TM)
        xs = _dispatch(dest2d, x1, jnp.zeros((nb * MOE_TB, d), F32))
        y = _experts(block_e, n_valid, xs, moe_w1[l], moe_w3[l], moe_w2[l])
        ln2 = (ln2_g[l].reshape(1, d), ln2_b[l].reshape(1, d))
        if l + 1 < depth:
            x, xb = _combine_ln(dest2d, gate, x1, *ln2, y, alpha, 0, t, True)
        else:
            outs = [_combine_ln(dest2d, gate, x1, *ln2, y, alpha, r0, b * s, False)[0]
                    for r0, b, s in segs]

    return (outs[0].reshape(x_prompt.shape), outs[1].reshape(x_sample.shape))
```

```python
import functools
import math

import jax
import jax.numpy as jnp
from jax import lax
from jax.experimental import pallas as pl
from jax.experimental.pallas import tpu as pltpu

F32 = jnp.float32
BF16 = jnp.bfloat16

GRID_W = 64
HEAD_DIM = 128
ROPE_THETA = 500000.0
DIFF_HEADS = 8
DIFF_QK_DIM = HEAD_DIM // 2
DIFF_ROT = DIFF_QK_DIM // 4
NA_HEADS = 8
NA_KH = 8
NA_KW = 16
MLA_HEADS = 8
MLA_Q_RANK = 512
MLA_KV_RANK = 256
MLA_NOPE = 128
MLA_ROPE = 64
MLA_V = 128
MLA_THETA = 10000.0
GQA_HEADS = 8
GQA_KV_HEADS = 2
AXIAL_THETA = 10000.0
N_GROUPS = 4
EXPERTS_PER_GROUP = 8
N_EXPERTS = N_GROUPS * EXPERTS_PER_GROUP
TOP_K = 2
LN_EPS = 1e-5
RMS_EPS = 1e-6
SUBLN_EPS = 1e-5

LANES = 128
V7X_VMEM_BYTES = 64 * 1024 * 1024
VMEM_LIMIT = V7X_VMEM_BYTES * 7 // 8

LOG2E = math.log2(math.e)
NEG_BIG = -1e30

PROJ_TM = 512
PROJ_TN = 1024
ROW_TM = 256
FLASH_ROWS = 512
FLASH_TK = 1024
FLASH_MAX_TILES = 16
NA_ROWS = 8
MOE_TB = 256
DMA_UNROLL = 8


def _cparams(sem):
    return pltpu.CompilerParams(dimension_semantics=sem, vmem_limit_bytes=VMEM_LIMIT)


def _rope_tables(pos, lane_freq, lane_kind):
    if pos.ndim == 1:
        pos = pos[:, None]
    ang = pos * lane_freq[None, :]
    cos, sin = jnp.cos(ang), jnp.sin(ang)
    kind = lane_kind[None, :]
    c = jnp.where(kind == 2, 1.0, cos)
    s1 = jnp.where(kind == 0, -sin, 0.0)
    s2 = jnp.where(kind == 1, sin, 0.0)
    return jnp.stack([c, s1, s2]).astype(F32)


def _inv_freq(dim, theta):
    return theta ** (-jnp.arange(0, dim, 2, dtype=F32) / dim)


def _diff_tables(pos):
    lane = jnp.arange(LANES)
    within = lane % DIFF_QK_DIM
    half = DIFF_ROT // 2
    inv = _inv_freq(DIFF_ROT, ROPE_THETA)
    freq = inv[within % half]
    kind = jnp.where(within < half, 0, jnp.where(within < DIFF_ROT, 1, 2))
    return _rope_tables(pos, freq, kind)


def _mla_tables(pos):
    lane = jnp.arange(LANES)
    half = MLA_ROPE // 2
    inv = _inv_freq(MLA_ROPE, MLA_THETA)
    freq = inv[lane % half]
    kind = jnp.where(lane < half, 0, jnp.where(lane < MLA_ROPE, 1, 2))
    return _rope_tables(pos, freq, kind)


def _axial_tables(pos):
    lane = jnp.arange(LANES)
    hd = HEAD_DIM // 2
    half = hd // 2
    inv = _inv_freq(hd, AXIAL_THETA)
    freq = inv[lane % half]
    kind = jnp.where((lane % hd) < half, 0, 1)
    p = jnp.where(lane[None, :] < hd, (pos // GRID_W)[:, None], (pos % GRID_W)[:, None])
    return _rope_tables(p.astype(F32), freq, kind)


def _apply_rope(x, c, s1, s2, half):
    return x * c + pltpu.roll(x, LANES - half, 1) * s1 + pltpu.roll(x, half, 1) * s2


def _even_proj_kernel(x_ref, w_ref, tab_ref, o_ref, *, half):
    j = pl.program_id(0)
    acc = jnp.dot(x_ref[...], w_ref[...], preferred_element_type=F32)

    @pl.when(j < 2)
    def _():
        c, s1, s2 = tab_ref[0, 0], tab_ref[0, 1], tab_ref[0, 2]
        for ci in range(acc.shape[1] // LANES):
            sl = slice(ci * LANES, (ci + 1) * LANES)
            o_ref[:, sl] = _apply_rope(acc[:, sl], c, s1, s2, half).astype(o_ref.dtype)

    @pl.when(j >= 2)
    def _():
        o_ref[...] = acc.astype(o_ref.dtype)


def _even_proj(xb, w, tabs):
    t, d = xb.shape
    n = w.shape[1]
    tm, tn = PROJ_TM, PROJ_TN
    assert t % tm == 0 and n % tn == 0 and DIFF_HEADS * HEAD_DIM == tn
    return pl.pallas_call(
        functools.partial(_even_proj_kernel, half=DIFF_ROT // 2),
        grid=(n // tn, t // tm),
        in_specs=[
            pl.BlockSpec((tm, d), lambda j, i: (i, 0)),
            pl.BlockSpec((d, tn), lambda j, i: (0, j)),
            pl.BlockSpec((1, 3, tm, LANES), lambda j, i: (jnp.minimum(j, 1), 0, i, 0)),
        ],
        out_specs=pl.BlockSpec((tm, tn), lambda j, i: (i, j)),
        out_shape=jax.ShapeDtypeStruct((t, n), BF16),
        compiler_params=_cparams(("arbitrary", "arbitrary")),
        name="even_proj",
    )(xb, w, tabs)


def _plain_proj_kernel(x_ref, w_ref, o_ref):
    o_ref[...] = jnp.dot(x_ref[...], w_ref[...], preferred_element_type=F32).astype(o_ref.dtype)


def _plain_proj(xb, w):
    t, d = xb.shape
    n = w.shape[1]
    tm = PROJ_TM
    assert t % tm == 0 and n % LANES == 0
    return pl.pallas_call(
        _plain_proj_kernel,
        grid=(t // tm,),
        in_specs=[pl.BlockSpec((tm, d), lambda i: (i, 0)),
                  pl.BlockSpec((d, n), lambda i: (0, 0))],
        out_specs=pl.BlockSpec((tm, n), lambda i: (i, 0)),
        out_shape=jax.ShapeDtypeStruct((t, n), BF16),
        compiler_params=_cparams(("arbitrary",)),
        name="odd_proj",
    )(xb, w)


def _flash_scratch(rows, dqk, dv, tk, seq):
    return [pltpu.VMEM((rows, dqk), BF16),
            pltpu.VMEM((seq, dv + LANES), BF16),
            pltpu.VMEM((2, rows, tk), F32),
            pltpu.VMEM((rows, LANES), F32),
            pltpu.VMEM((rows, dv + LANES), F32)]


def _flash_loop(first, q_ref, k_ref, v_ref, vaug_ref, s_ref, m_ref, acc_ref, tk):
    n = k_ref.shape[0] // tk
    assert n <= FLASH_MAX_TILES
    nc = tk // LANES
    dv = v_ref.shape[1]

    @pl.when(first)
    def _():
        lane = lax.broadcasted_iota(jnp.int32, (v_ref.shape[0], LANES), 1)
        vaug_ref[:, 0:dv] = v_ref[...]
        vaug_ref[:, dv:] = jnp.where(lane == 0, 1.0, 0.0).astype(BF16)

    def scores(j):
        s_ref[j % 2] = lax.dot_general(q_ref[...], k_ref[j * tk:(j + 1) * tk, :],
                                       (((1,), (1,)), ((), ())), preferred_element_type=F32)

    def softmax_pv(j):
        slot = j % 2
        cmax = s_ref[slot, :, 0:LANES]
        for c in range(1, nc):
            cmax = jnp.maximum(cmax, s_ref[slot, :, c * LANES:(c + 1) * LANES])
        m_prev = m_ref[...]
        m_new = jnp.maximum(m_prev, jnp.max(cmax, axis=1, keepdims=True))
        a = jnp.exp2(m_prev - m_new)
        p = jnp.concatenate(
            [jnp.exp2(s_ref[slot, :, c * LANES:(c + 1) * LANES] - m_new).astype(BF16)
             for c in range(nc)], axis=1)
        pv = jnp.dot(p, vaug_ref[j * tk:(j + 1) * tk, :], preferred_element_type=F32)
        acc_ref[...] = jnp.concatenate([a, a], axis=1) * acc_ref[...] + pv
        m_ref[...] = m_new

    m_ref[...] = jnp.full(m_ref.shape, -jnp.inf, F32)
    acc_ref[...] = jnp.zeros(acc_ref.shape, F32)
    scores(0)
    for j in range(n):
        if j + 1 < n:
            scores(j + 1)
        softmax_pv(j)
    return acc_ref[:, 0:dv], acc_ref[:, dv:dv + 1]


def _segment_call(kernel_fn, *, grid, in_specs, args, out_spec, out_cols, rows, scratch, name):
    return pl.pallas_call(
        kernel_fn,
        grid=grid,
        in_specs=in_specs,
        out_specs=out_spec,
        out_shape=jax.ShapeDtypeStruct((rows, out_cols), BF16),
        scratch_shapes=scratch,
        compiler_params=_cparams(("arbitrary",) * len(grid)),
        name=name,
    )(*args)


def _flash_tk(seq):
    return min(FLASH_TK, seq // 2)


def _diff_attn_kernel(lam_ref, g_ref, q_ref, k_ref, v_ref, o_ref, q_scr, *scr, tk, lambda_init):
    tq = q_ref.shape[0]
    q = q_ref[...]
    lane = lax.broadcasted_iota(jnp.int32, q.shape, 1)
    zero = jnp.zeros_like(q)
    q_scr[0:tq] = jnp.where(lane < DIFF_QK_DIM, q, zero)
    q_scr[tq:2 * tq] = jnp.where(lane >= DIFF_QK_DIM, q, zero)
    acc, l = _flash_loop(pl.program_id(2) == 0, q_scr, k_ref, v_ref, *scr, tk)
    o = acc / l
    lv = lam_ref[...].astype(F32)
    lam = (jnp.exp(jnp.sum(lv[0:1] * lv[1:2], axis=1, keepdims=True))
           - jnp.exp(jnp.sum(lv[2:3] * lv[3:4], axis=1, keepdims=True)) + lambda_init)
    o = o[:tq] - lam * o[tq:]
    y = o * lax.rsqrt(jnp.mean(o * o, axis=1, keepdims=True) + SUBLN_EPS)
    o_ref[...] = (y * g_ref[...] * (1.0 - lambda_init)).astype(o_ref.dtype)


def _diff_attention(h, lam_vec, subln_g, row0, batch, seq, lambda_init):
    tq = FLASH_ROWS // 2
    tk = _flash_tk(seq)
    assert seq % tq == 0 and seq % tk == 0 and row0 % seq == 0
    nq = seq // tq
    hcols = DIFF_HEADS
    qrow = lambda b, hh, i: (row0 // tq + b * nq + i, hh)
    return _segment_call(
        functools.partial(_diff_attn_kernel, tk=tk, lambda_init=lambda_init),
        grid=(batch, DIFF_HEADS, nq),
        in_specs=[
            pl.BlockSpec((4, DIFF_QK_DIM), lambda b, hh, i: (0, 0)),
            pl.BlockSpec((1, HEAD_DIM), lambda b, hh, i: (0, 0)),
            pl.BlockSpec((tq, HEAD_DIM), qrow),
            pl.BlockSpec((seq, HEAD_DIM), lambda b, hh, i: (row0 // seq + b, hcols + hh)),
            pl.BlockSpec((seq, HEAD_DIM), lambda b, hh, i: (row0 // seq + b, 2 * hcols + hh)),
        ],
        args=[lam_vec, subln_g.reshape(1, HEAD_DIM), h, h, h],
        out_spec=pl.BlockSpec((tq, HEAD_DIM), lambda b, hh, i: (b * nq + i, hh)),
        out_cols=DIFF_HEADS * HEAD_DIM, rows=batch * seq,
        scratch=_flash_scratch(2 * tq, HEAD_DIM, HEAD_DIM, tk, seq), name="diff_attn")


def _mla_attn_kernel(q_ref, k_ref, v_ref, o_ref, q_scr, *scr, tk):
    del q_scr
    acc, l = _flash_loop(pl.program_id(2) == 0, q_ref, k_ref, v_ref, *scr, tk)
    o_ref[...] = (acc / l).astype(o_ref.dtype)


def _mla_attention(qm, km, vm, row0, batch, seq):
    tq = FLASH_ROWS
    tk = _flash_tk(seq)
    assert seq % tq == 0 and seq % tk == 0 and row0 % seq == 0
    nq = seq // tq
    dqk = 2 * LANES
    qrow = lambda b, hh, i: (row0 // tq + b * nq + i, hh)
    return _segment_call(
        functools.partial(_mla_attn_kernel, tk=tk),
        grid=(batch, MLA_HEADS, nq),
        in_specs=[
            pl.BlockSpec((tq, dqk), qrow),
            pl.BlockSpec((seq, dqk), lambda b, hh, i: (row0 // seq + b, hh)),
            pl.BlockSpec((seq, MLA_V), lambda b, hh, i: (row0 // seq + b, hh)),
        ],
        args=[qm, km, vm],
        out_spec=pl.BlockSpec((tq, MLA_V), lambda b, hh, i: (b * nq + i, hh)),
        out_cols=MLA_HEADS * MLA_V, rows=batch * seq,
        scratch=_flash_scratch(tq, 8, MLA_V, tk, seq), name="mla_attn")


def _gqa_attn_kernel(q_ref, k_ref, v_ref, o_ref, q_scr, *scr, tk, group):
    tq = q_ref.shape[0]
    for g in range(group):
        q_scr[g * tq:(g + 1) * tq] = q_ref[:, g * HEAD_DIM:(g + 1) * HEAD_DIM]
    acc, l = _flash_loop(pl.program_id(2) == 0, q_scr, k_ref, v_ref, *scr, tk)
    o = acc / l
    for g in range(group):
        o_ref[:, g * HEAD_DIM:(g + 1) * HEAD_DIM] = o[g * tq:(g + 1) * tq].astype(o_ref.dtype)


def _gqa_attention(qg, kg, h, v_col0, row0, batch, seq):
    group = GQA_HEADS // GQA_KV_HEADS
    tq = FLASH_ROWS // group
    tk = _flash_tk(seq)
    assert seq % tq == 0 and seq % tk == 0 and row0 % seq == 0
    nq = seq // tq
    qrow = lambda b, n, i: (row0 // tq + b * nq + i, n)
    return _segment_call(
        functools.partial(_gqa_attn_kernel, tk=tk, group=group),
        grid=(batch, GQA_KV_HEADS, nq),
        in_specs=[
            pl.BlockSpec((tq, group * HEAD_DIM), qrow),
            pl.BlockSpec((seq, HEAD_DIM), lambda b, n, i: (row0 // seq + b, n)),
            pl.BlockSpec((seq, HEAD_DIM), lambda b, n, i: (row0 // seq + b, v_col0 + n)),
        ],
        args=[qg, kg, h],
        out_spec=pl.BlockSpec((tq, group * HEAD_DIM), lambda b, n, i: (b * nq + i, n)),
        out_cols=GQA_HEADS * HEAD_DIM, rows=batch * seq,
        scratch=_flash_scratch(group * tq, HEAD_DIM, HEAD_DIM, tk, seq), name="gqa_attn")


def _na_bias_table(rpb):
    cols = jnp.arange(GRID_W)
    start = jnp.clip(cols - NA_KW // 2, 0, GRID_W - NA_KW)
    kc = cols[None, :]
    valid = (kc >= start[:, None]) & (kc < start[:, None] + NA_KW)
    dc = jnp.clip(kc - cols[:, None] + (NA_KW - 1), 0, 2 * NA_KW - 2)
    tab = rpb.astype(F32)[:, :, dc]
    tab = jnp.where(valid[None, None], tab, NEG_BIG)
    return jnp.concatenate([tab[:, :-1], tab[:, 1:]], axis=-1) * LOG2E


def _na_kernel(q_ref, k0_ref, k1_ref, k2_ref, k3_ref, v0_ref, v1_ref, v2_ref, v3_ref, bias_ref, o_ref,
               *, rows, scale):
    i = pl.program_id(1)
    k_refs = (k0_ref, k1_ref, k2_ref, k3_ref)
    v_refs = (v0_ref, v1_ref, v2_ref, v3_ref)
    key_row0 = jnp.clip(i * NA_ROWS - NA_KH // 2, 0, rows - 2 * NA_ROWS)
    d = i * NA_ROWS - key_row0
    n_pairs = NA_ROWS

    lane = lax.broadcasted_iota(jnp.int32, (8, LANES), 1)
    row_mask = []
    for j in range(NA_ROWS):
        first = jnp.clip(j + d - NA_KH // 2, 0, 2 * NA_ROWS - NA_KH)
        per_pair = []
        for m in range(n_pairs):
            key_row = 2 * m + jnp.where(lane >= GRID_W, 1, 0)
            madd = jnp.where(key_row >= first, jnp.where(key_row < first + NA_KH, 0.0, NEG_BIG), NEG_BIG)
            per_pair.append(jnp.concatenate([madd] * (GRID_W // 8), axis=0))
        row_mask.append(per_pair)

    for hh in range(NA_HEADS):
        cs = slice(hh * HEAD_DIM, (hh + 1) * HEAD_DIM)
        q = (q_ref[:, cs].astype(F32) * scale).astype(BF16)
        s = [lax.dot_general(q, kr[:, cs], (((1,), (1,)), ((), ())), preferred_element_type=F32)
             for kr in k_refs]
        cols = []
        for m in range(n_pairs):
            c, half = divmod(m, n_pairs // 4)
            pieces = []
            for j in range(NA_ROWS):
                dr = jnp.clip(2 * m - j - d + (NA_KH - 1), 0, 2 * NA_KH - 3)
                sb = s[c][j * GRID_W:(j + 1) * GRID_W, half * LANES:(half + 1) * LANES]
                pieces.append(sb + bias_ref[hh, dr] + row_mask[j][m])
            cols.append(jnp.concatenate(pieces, axis=0))
        cmax = cols[0]
        for col in cols[1:]:
            cmax = jnp.maximum(cmax, col)
        mx = jnp.max(cmax, axis=1, keepdims=True)
        ps = [jnp.exp2(col - mx) for col in cols]
        psum = ps[0]
        for p in ps[1:]:
            psum = psum + p
        l = jnp.sum(psum, axis=1, keepdims=True)
        per_q = n_pairs // 4
        o = None
        for c, vr in enumerate(v_refs):
            pc = jnp.concatenate([p.astype(BF16) for p in ps[c * per_q:(c + 1) * per_q]], axis=1)
            oc = jnp.dot(pc, vr[:, cs], preferred_element_type=F32)
            o = oc if o is None else o + oc
        o_ref[:, cs] = (o / l).astype(o_ref.dtype)


def _na_attention(h, bias_tab, col0, row0, batch, seq):
    rows = seq // GRID_W
    assert NA_ROWS == NA_KH and NA_ROWS % 4 == 0
    assert seq % GRID_W == 0 and rows % NA_ROWS == 0 and rows >= 2 * NA_ROWS
    nb = rows // NA_ROWS
    blk = NA_ROWS * GRID_W
    qtr = blk // 2
    width = NA_HEADS * HEAD_DIM
    assert row0 % blk == 0
    qbase = row0 // blk
    kbase = row0 // qtr
    per_seq = seq // qtr

    def win(c, col):
        def index(b, i):
            first = jnp.clip(2 * i - 1, 0, per_seq - 4)
            return (kbase + b * per_seq + first + c, col)
        return pl.BlockSpec((qtr, width), index)

    qspec = lambda col: pl.BlockSpec((blk, width), lambda b, i: (qbase + b * nb + i, col))
    return _segment_call(
        functools.partial(_na_kernel, rows=rows, scale=HEAD_DIM ** -0.5 * LOG2E),
        grid=(batch, nb),
        in_specs=[qspec(col0)] + [win(c, col0 + 1) for c in range(4)] + [win(c, col0 + 2) for c in range(4)]
                 + [pl.BlockSpec(bias_tab.shape, lambda b, i: (0, 0, 0, 0))],
        args=[h] * 9 + [bias_tab],
        out_spec=pl.BlockSpec((blk, width), lambda b, i: (b * nb + i, 0)),
        out_cols=width, rows=batch * seq,
        scratch=[], name="na_attn")


def _rms(x, g, eps):
    return x * lax.rsqrt(jnp.mean(x * x, axis=1, keepdims=True) + eps) * g


def _odd_prep_kernel(h_ref, wq_ref, wk_ref, wv_ref, qn_ref, kvn_ref, gqn_ref, gkn_ref,
                     mtab_ref, atab_ref, qm_ref, km_ref, vm_ref, qg_ref, kg_ref, *, cols):
    c_qc, c_kvc, c_gq, c_gk, c_kpe = cols
    mla_scale = (MLA_NOPE + MLA_ROPE) ** -0.5 * LOG2E
    gqa_scale = HEAD_DIM ** -0.5 * LOG2E
    mc, ms1, ms2 = mtab_ref[0], mtab_ref[1], mtab_ref[2]
    ac, as1, as2 = atab_ref[0], atab_ref[1], atab_ref[2]
    mhalf = MLA_ROPE // 2
    ahalf = HEAD_DIM // 4

    qc = _rms(h_ref[:, c_qc:c_qc + MLA_Q_RANK].astype(F32), qn_ref[...], RMS_EPS)
    q = jnp.dot(qc.astype(BF16), wq_ref[...], preferred_element_type=F32)
    for hh in range(MLA_HEADS):
        lo = hh * 2 * LANES
        qm_ref[:, lo:lo + LANES] = (q[:, lo:lo + LANES] * mla_scale).astype(qm_ref.dtype)
        qr = _apply_rope(q[:, lo + LANES:lo + 2 * LANES], mc, ms1, ms2, mhalf)
        qm_ref[:, lo + LANES:lo + 2 * LANES] = (qr * mla_scale).astype(qm_ref.dtype)

    kvc = _rms(h_ref[:, c_kvc:c_kvc + MLA_KV_RANK].astype(F32), kvn_ref[...], RMS_EPS).astype(BF16)
    kn = jnp.dot(kvc, wk_ref[...], preferred_element_type=F32)
    vm_ref[...] = jnp.dot(kvc, wv_ref[...], preferred_element_type=F32).astype(vm_ref.dtype)
    kpe = _apply_rope(h_ref[:, c_kpe:c_kpe + LANES].astype(F32), mc, ms1, ms2, mhalf)
    kpe = kpe.astype(km_ref.dtype)
    for hh in range(MLA_HEADS):
        lo = hh * 2 * LANES
        km_ref[:, lo:lo + LANES] = kn[:, hh * LANES:(hh + 1) * LANES].astype(km_ref.dtype)
        km_ref[:, lo + LANES:lo + 2 * LANES] = kpe

    for hh in range(GQA_HEADS):
        sl = slice(c_gq + hh * HEAD_DIM, c_gq + (hh + 1) * HEAD_DIM)
        x = _rms(h_ref[:, sl].astype(F32), gqn_ref[...], RMS_EPS)
        x = _apply_rope(x, ac, as1, as2, ahalf) * gqa_scale
        qg_ref[:, hh * HEAD_DIM:(hh + 1) * HEAD_DIM] = x.astype(qg_ref.dtype)
    for hh in range(GQA_KV_HEADS):
        sl = slice(c_gk + hh * HEAD_DIM, c_gk + (hh + 1) * HEAD_DIM)
        x = _rms(h_ref[:, sl].astype(F32), gkn_ref[...], RMS_EPS)
        x = _apply_rope(x, ac, as1, as2, ahalf)
        kg_ref[:, hh * HEAD_DIM:(hh + 1) * HEAD_DIM] = x.astype(kg_ref.dtype)


def _odd_prep(h, wq, wk, wv, qn, kvn, gqn, gkn, mtab, atab, cols):
    t, n = h.shape
    tm = ROW_TM
    assert t % tm == 0
    row = lambda i: (i, 0)
    full = lambda i: (0, 0)
    tab = lambda i: (0, i, 0)
    widths = (MLA_HEADS * 2 * LANES, MLA_HEADS * 2 * LANES, MLA_HEADS * MLA_V,
              GQA_HEADS * HEAD_DIM, GQA_KV_HEADS * HEAD_DIM)
    return pl.pallas_call(
        functools.partial(_odd_prep_kernel, cols=cols),
        grid=(t // tm,),
        in_specs=[pl.BlockSpec((tm, n), row),
                  pl.BlockSpec(wq.shape, full), pl.BlockSpec(wk.shape, full), pl.BlockSpec(wv.shape, full),
                  pl.BlockSpec(qn.shape, full), pl.BlockSpec(kvn.shape, full),
                  pl.BlockSpec(gqn.shape, full), pl.BlockSpec(gkn.shape, full),
                  pl.BlockSpec((3, tm, LANES), tab), pl.BlockSpec((3, tm, LANES), tab)],
        out_specs=[pl.BlockSpec((tm, w), row) for w in widths],
        out_shape=[jax.ShapeDtypeStruct((t, w), BF16) for w in widths],
        compiler_params=_cparams(("arbitrary",)),
        name="odd_prep",
    )(h, wq, wk, wv, qn, kvn, gqn, gkn, mtab, atab)


def _layer_norm(y, g, b):
    mu = jnp.mean(y, axis=1, keepdims=True)
    yc = y - mu
    var = jnp.mean(yc * yc, axis=1, keepdims=True)
    return yc * lax.rsqrt(var + LN_EPS) * g + b


def _route(logits):
    lane = lax.broadcasted_iota(jnp.int32, logits.shape, 1)
    lanef = lane.astype(F32)
    big = float(LANES)
    gmask = lane < N_GROUPS
    gl = jnp.where(gmask, logits, -jnp.inf)
    gmax = jnp.max(gl, axis=1, keepdims=True)
    gsel = jnp.min(jnp.where(gl == gmax, lanef, big), axis=1, keepdims=True)
    gden = jnp.sum(jnp.where(gmask, jnp.exp(gl - gmax), 0.0), axis=1, keepdims=True)
    gprob = 1.0 / gden
    lo = N_GROUPS + EXPERTS_PER_GROUP * gsel
    emask = (lanef >= lo) & (lanef < lo + EXPERTS_PER_GROUP)
    el = jnp.where(emask, logits, -jnp.inf)
    v1 = jnp.max(el, axis=1, keepdims=True)
    i1 = jnp.min(jnp.where(el == v1, lanef, big), axis=1, keepdims=True)
    el2 = jnp.where(lanef == i1, -jnp.inf, el)
    v2 = jnp.max(el2, axis=1, keepdims=True)
    i2 = jnp.min(jnp.where(el2 == v2, lanef, big), axis=1, keepdims=True)
    e = jnp.exp(v2 - v1)
    g1 = gprob / (1.0 + e)
    g2 = gprob * e / (1.0 + e)
    gate = jnp.where(lane == 0, g1, jnp.where(lane == 1, g2, 0.0))
    return i1 - N_GROUPS, i2 - N_GROUPS, gate


def _rank_in_expert(e1, e2, cnt_ref):
    tm = e1.shape[0]
    lanef = lax.broadcasted_iota(jnp.int32, (tm, LANES), 1).astype(F32)
    sel1 = lanef == e1
    sel2 = lanef == e2
    both = jnp.where(sel1, 1.0, jnp.where(sel2, 1.0, 0.0))
    row = lax.broadcasted_iota(jnp.int32, (tm, tm), 0)
    col = lax.broadcasted_iota(jnp.int32, (tm, tm), 1)
    ltri = jnp.where(col < row, 1.0, 0.0).astype(BF16)
    before = jnp.dot(ltri, both.astype(BF16), preferred_element_type=F32) + cnt_ref[...]
    r1 = jnp.sum(jnp.where(sel1, before, 0.0), axis=1, keepdims=True)
    r2 = jnp.sum(jnp.where(sel2, before, 0.0), axis=1, keepdims=True)
    cnt_ref[...] = cnt_ref[...] + jnp.sum(both, axis=0, keepdims=True)
    return r1, r2


def _pick_part(refs, ends, step):
    x = refs[-1][...]
    for ref, end in zip(refs[-2::-1], ends[-2::-1]):
        x = jnp.where(step < end, ref[...], x)
    return x


def _out_ln_route_kernel(*refs, alpha, ends):
    n1, n2, n3 = (len(e) for e in ends)
    a1_refs, a2_refs, x_refs = refs[:n1], refs[n1:n1 + n2], refs[n1 + n2:n1 + n2 + n3]
    w_ref, g_ref, b_ref, wr_ref, br_ref, x1_ref, meta_ref, gate_ref, cnt_ref = refs[n1 + n2 + n3:]
    step = pl.program_id(0)

    @pl.when(step == 0)
    def _():
        cnt_ref[...] = jnp.zeros(cnt_ref.shape, F32)

    a1 = _pick_part(a1_refs, ends[0], step)
    a2 = _pick_part(a2_refs, ends[1], step)
    x = _pick_part(x_refs, ends[2], step)
    k1 = a1.shape[1]
    mix = (jnp.dot(a1, w_ref[0:k1, :], preferred_element_type=F32)
           + jnp.dot(a2, w_ref[k1:, :], preferred_element_type=F32))
    x1 = _layer_norm(alpha * x + mix, g_ref[...], b_ref[...])
    x1_ref[...] = x1
    x_hi = x1.astype(BF16)
    x_mid = (x1 - x_hi.astype(F32)).astype(BF16)
    t = (jnp.dot(x_hi, wr_ref[...], preferred_element_type=F32)
         + jnp.dot(x_mid, wr_ref[...], preferred_element_type=F32))
    logits = t[:, :LANES] + t[:, LANES:] + br_ref[...]
    e1, e2, gate = _route(logits)
    r1, r2 = _rank_in_expert(e1, e2, cnt_ref)
    lane = lax.broadcasted_iota(jnp.int32, gate.shape, 1)
    meta = jnp.where(lane == 0, e1, jnp.where(lane == 1, e2,
                                              jnp.where(lane == 2, r1, jnp.where(lane == 3, r2, 0.0))))
    meta_ref[...] = meta.astype(jnp.int32)
    gate_ref[...] = gate


def _part_specs(parts, tm):
    specs, ends, start = [], [], 0
    for p in parts:
        assert p.shape[0] % tm == 0
        n = p.shape[0] // tm
        specs.append(pl.BlockSpec((tm, p.shape[1]), functools.partial(
            lambda i, start, n: (jnp.clip(i - start, 0, n - 1), 0), start=start, n=n)))
        start += n
        ends.append(start)
    return specs, tuple(ends)


def _out_ln_route(a1_parts, a2_parts, x_parts, w, g, b, wr, br, alpha):
    d = x_parts[0].shape[1]
    t = sum(xp.shape[0] for xp in x_parts)
    tm = ROW_TM
    row = lambda i: (i, 0)
    full = lambda i: (0, 0)
    spec_lists, ends = zip(*(_part_specs(parts, tm) for parts in (a1_parts, a2_parts, x_parts)))
    assert all(e[-1] == t // tm for e in ends)
    return pl.pallas_call(
        functools.partial(_out_ln_route_kernel, alpha=alpha, ends=ends),
        grid=(t // tm,),
        in_specs=[*spec_lists[0], *spec_lists[1], *spec_lists[2],
                  pl.BlockSpec(w.shape, full),
                  pl.BlockSpec((1, d), full), pl.BlockSpec((1, d), full),
                  pl.BlockSpec(wr.shape, full), pl.BlockSpec((1, LANES), full)],
        out_specs=[pl.BlockSpec((tm, d), row), pl.BlockSpec((tm, LANES), row),
                   pl.BlockSpec((tm, LANES), row), pl.BlockSpec((1, LANES), full)],
        out_shape=[jax.ShapeDtypeStruct((t, d), F32), jax.ShapeDtypeStruct((t, LANES), jnp.int32),
                   jax.ShapeDtypeStruct((t, LANES), F32), jax.ShapeDtypeStruct((1, LANES), F32)],
        compiler_params=_cparams(("arbitrary",)),
        name="out_ln_route",
    )(*a1_parts, *a2_parts, *x_parts, w, g, b, wr, br)


def _row_copy(src_ref, src_row, dst_ref, dst_row, sem):
    return pltpu.make_async_copy(src_ref.at[pl.ds(src_row, 1)], dst_ref.at[pl.ds(dst_row, 1)], sem)


def _dispatch_kernel(dest_ref, x_ref, xs_in_ref, xs_ref, sem):
    del xs_in_ref
    tm = x_ref.shape[0]

    def start(r, _):
        for k in range(TOP_K):
            _row_copy(x_ref, r, xs_ref, dest_ref[0, 0, TOP_K * r + k], sem).start()
        return 0

    def wait(r, _):
        for k in range(TOP_K):
            _row_copy(x_ref, r, xs_ref, dest_ref[0, 0, TOP_K * r + k], sem).wait()
        return 0

    lax.fori_loop(0, tm, start, 0, unroll=DMA_UNROLL)
    lax.fori_loop(0, tm, wait, 0, unroll=DMA_UNROLL)


def _dispatch(dest2d, x, xs_init):
    t, d = x.shape
    tm = ROW_TM
    return pl.pallas_call(
        _dispatch_kernel,
        grid=(t // tm,),
        in_specs=[pl.BlockSpec((1, 1, TOP_K * tm), lambda i: (i, 0, 0), memory_space=pltpu.SMEM),
                  pl.BlockSpec((tm, d), lambda i: (i, 0)),
                  pl.BlockSpec(memory_space=pl.ANY)],
        out_specs=pl.BlockSpec(memory_space=pl.ANY),
        out_shape=jax.ShapeDtypeStruct(xs_init.shape, xs_init.dtype),
        scratch_shapes=[pltpu.SemaphoreType.DMA(())],
        input_output_aliases={2: 0},
        compiler_params=_cparams(("arbitrary",)),
        name="moe_dispatch",
    )(dest2d, x, xs_init)


def _expert_kernel(be_ref, nv_ref, xs_ref, w1_ref, w3_ref, w2_ref, y_ref, w1b, w3b, w2b):
    b = pl.program_id(0)
    valid = b < nv_ref[0]
    new_expert = jnp.logical_or(b == 0, be_ref[b] != be_ref[jnp.maximum(b - 1, 0)])

    @pl.when(jnp.logical_and(valid, new_expert))
    def _():
        w1b[...] = w1_ref[0, 0].astype(BF16)
        w3b[...] = w3_ref[0, 0].astype(BF16)
        w2b[...] = w2_ref[0, 0].astype(BF16)

    @pl.when(valid)
    def _():
        x = xs_ref[...].astype(BF16)
        h1 = jnp.dot(x, w1b[...], preferred_element_type=F32)
        h3 = jnp.dot(x, w3b[...], preferred_element_type=F32)
        hm = (h1 / (1.0 + jnp.exp(-h1)) * h3).astype(BF16)
        y_ref[...] = jnp.dot(hm, w2b[...], preferred_element_type=F32).astype(y_ref.dtype)

    @pl.when(jnp.logical_not(valid))
    def _():
        y_ref[...] = jnp.zeros_like(y_ref)


def _experts(block_e, n_valid, xs, w1, w3, w2, layer):
    p, d = xs.shape
    tb = MOE_TB
    nb = p // tb
    de = w1.shape[3]
    last = lambda b, nv: jnp.minimum(b, nv[0] - 1)
    wmap = lambda b, be, nv: (layer, be[last(b, nv)], 0, 0)
    return pl.pallas_call(
        _expert_kernel,
        grid_spec=pltpu.PrefetchScalarGridSpec(
            num_scalar_prefetch=2,
            grid=(nb,),
            in_specs=[pl.BlockSpec((tb, d), lambda b, be, nv: (last(b, nv), 0)),
                      pl.BlockSpec((1, 1, d, de), wmap), pl.BlockSpec((1, 1, d, de), wmap),
                      pl.BlockSpec((1, 1, de, d), wmap)],
            out_specs=pl.BlockSpec((tb, d), lambda b, be, nv: (b, 0)),
            scratch_shapes=[pltpu.VMEM((d, de), BF16), pltpu.VMEM((d, de), BF16),
                            pltpu.VMEM((de, d), BF16)],
        ),
        out_shape=jax.ShapeDtypeStruct((p, d), F32),
        compiler_params=_cparams(("arbitrary",)),
        name="moe_experts",
    )(block_e, n_valid, xs, w1, w3, w2)


def _combine_ln_kernel(dest_ref, gate_ref, x_ref, g_ref, b_ref, y_ref, o_ref, *rest, alpha):
    *maybe_ob_ref, buf, sem = rest
    tm = x_ref.shape[0]

    def start(r, _):
        for k in range(TOP_K):
            _row_copy(y_ref, dest_ref[0, 0, TOP_K * r + k], buf.at[k], r, sem).start()
        return 0

    def wait(r, _):
        for k in range(TOP_K):
            _row_copy(y_ref, dest_ref[0, 0, TOP_K * r + k], buf.at[k], r, sem).wait()
        return 0

    lax.fori_loop(0, tm, start, 0, unroll=DMA_UNROLL)
    lax.fori_loop(0, tm, wait, 0, unroll=DMA_UNROLL)
    gate = gate_ref[...]
    ff = buf[0] * gate[:, 0:1] + buf[1] * gate[:, 1:2]
    x2 = _layer_norm(alpha * x_ref[...] + ff, g_ref[...], b_ref[...])
    o_ref[...] = x2
    for ob_ref in maybe_ob_ref:
        ob_ref[...] = x2.astype(ob_ref.dtype)


def _combine_ln(dest2d, gate, x, g, b, y, alpha, row0, nrows, with_bf16):
    d = x.shape[1]
    tm = ROW_TM
    assert row0 % tm == 0 and nrows % tm == 0
    base = row0 // tm
    row_in = lambda i: (base + i, 0)
    row_out = lambda i: (i, 0)
    full = lambda i: (0, 0)
    n_out = 2 if with_bf16 else 1
    return pl.pallas_call(
        functools.partial(_combine_ln_kernel, alpha=alpha),
        grid=(nrows // tm,),
        in_specs=[pl.BlockSpec((1, 1, TOP_K * tm), lambda i: (base + i, 0, 0), memory_space=pltpu.SMEM),
                  pl.BlockSpec((tm, LANES), row_in), pl.BlockSpec((tm, d), row_in),
                  pl.BlockSpec((1, d), full), pl.BlockSpec((1, d), full),
                  pl.BlockSpec(memory_space=pl.ANY)],
        out_specs=[pl.BlockSpec((tm, d), row_out)] * n_out,
        out_shape=[jax.ShapeDtypeStruct((nrows, d), F32), jax.ShapeDtypeStruct((nrows, d), BF16)][:n_out],
        scratch_shapes=[pltpu.VMEM((TOP_K, tm, d), F32), pltpu.SemaphoreType.DMA(())],
        compiler_params=_cparams(("arbitrary",)),
        name="moe_combine_ln",
    )(dest2d, gate, x, g, b, y)


def _moe_plan(eid, rank, counts, tb):
    a = eid.shape[0] * TOP_K
    padded = (counts + tb - 1) // tb * tb
    pend = jnp.cumsum(padded)
    pstart = pend - padded
    dest = (jnp.take(pstart, eid) + rank).astype(jnp.int32)
    nb = -(-a // tb) + N_EXPERTS
    block_row0 = jnp.arange(nb, dtype=jnp.int32) * tb
    block_e = jnp.minimum(jnp.sum((pend[None, :] <= block_row0[:, None]).astype(jnp.int32), axis=1),
                          N_EXPERTS - 1).astype(jnp.int32)
    n_valid = (pend[-1] // tb).astype(jnp.int32).reshape(1)
    return dest, block_e, n_valid, nb


def _segments(groups):
    segs, row0 = [], 0
    for b, s in groups:
        segs.append((row0, b, s))
        row0 += b * s
    return segs


def kernel(x_prompt, x_sample, ev_w_in, ev_w_out, diff_lambda, diff_subln, na_rpb, od_w_in, od_w_out,
           mla_q_norm, mla_w_q_up, mla_kv_norm, mla_w_kv_up, gqa_q_norm, gqa_k_norm,
           ln1_g, ln1_b, ln2_g, ln2_b, moe_w_group, moe_b_group, moe_w_expert, moe_b_expert,
           moe_w1, moe_w3, moe_w2):
    d = x_prompt.shape[-1]
    depth = ln1_g.shape[0]
    alpha = (2 * depth) ** 0.25
    groups = [(x_prompt.shape[0], x_prompt.shape[1]), (x_sample.shape[0], x_sample.shape[1])]
    segs = _segments(groups)
    x_parts = [x_prompt.reshape(-1, d), x_sample.reshape(-1, d)]
    t = sum(xp.shape[0] for xp in x_parts)
    xb = jnp.concatenate([xp.astype(BF16) for xp in x_parts], axis=0)
    pos = jnp.concatenate([jnp.tile(jnp.arange(s, dtype=jnp.int32), b) for b, s in groups])
    posf = pos.astype(F32)

    for l in range(depth):
        i = l // 2
        if l % 2 == 0:
            tab = _diff_tables(posf)
            qscale = DIFF_QK_DIM ** -0.5 * LOG2E
            tabs = jnp.stack([tab * qscale, tab])
            h = _even_proj(xb, ev_w_in[i].astype(BF16), tabs)
            lambda_init = 0.8 - 0.6 * math.exp(-0.3 * l)
            bias_tab = _na_bias_table(na_rpb[i])
            o_a = [_diff_attention(h, diff_lambda[i], diff_subln[i], r0, b, s, lambda_init)
                   for r0, b, s in segs]
            o_b = [_na_attention(h, bias_tab, 3, r0, b, s) for r0, b, s in segs]
            w_out = ev_w_out[i]
        else:
            s1 = MLA_Q_RANK
            s2 = s1 + MLA_KV_RANK
            s3 = s2 + MLA_ROPE
            w_in = od_w_in[i]
            w_perm = jnp.concatenate(
                [w_in[:, :s2], w_in[:, s3:], w_in[:, s2:s3], jnp.zeros((d, LANES - MLA_ROPE), w_in.dtype)],
                axis=1).astype(BF16)
            c_gq = s2
            c_gk = c_gq + GQA_HEADS * HEAD_DIM
            c_gv = c_gk + GQA_KV_HEADS * HEAD_DIM
            c_kpe = c_gv + GQA_KV_HEADS * HEAD_DIM
            h = _plain_proj(xb, w_perm)
            wq = mla_w_q_up[i].reshape(MLA_Q_RANK, MLA_HEADS, MLA_NOPE + MLA_ROPE)
            wq = jnp.pad(wq, ((0, 0), (0, 0), (0, 2 * LANES - MLA_NOPE - MLA_ROPE)))
            wq = wq.reshape(MLA_Q_RANK, MLA_HEADS * 2 * LANES).astype(BF16)
            wkv = mla_w_kv_up[i].reshape(MLA_KV_RANK, MLA_HEADS, MLA_NOPE + MLA_V)
            wk = wkv[:, :, :MLA_NOPE].reshape(MLA_KV_RANK, MLA_HEADS * MLA_NOPE).astype(BF16)
            wv = wkv[:, :, MLA_NOPE:].reshape(MLA_KV_RANK, MLA_HEADS * MLA_V).astype(BF16)
            qm, km, vm, qg, kg = _odd_prep(
                h, wq, wk, wv, mla_q_norm[i].reshape(1, -1), mla_kv_norm[i].reshape(1, -1),
                gqa_q_norm[i].reshape(1, -1), gqa_k_norm[i].reshape(1, -1),
                _mla_tables(posf), _axial_tables(pos), (0, s1, c_gq, c_gk, c_kpe))
            o_a = [_mla_attention(qm, km, vm, r0, b, s) for r0, b, s in segs]
            o_b = [_gqa_attention(qg, kg, h, c_gv // HEAD_DIM, r0, b, s) for r0, b, s in segs]
            w_out = od_w_out[i]

        wr = jnp.concatenate([moe_w_group[l], moe_w_expert[l]], axis=1)
        wr = jnp.pad(wr, ((0, 0), (0, LANES - wr.shape[1])))
        wr_hi = wr.astype(BF16)
        wr = jnp.concatenate([wr_hi, (wr - wr_hi.astype(F32)).astype(BF16)], axis=1)
        br = jnp.concatenate([moe_b_group[l], moe_b_expert[l]])
        br = jnp.pad(br, (0, LANES - br.shape[0])).reshape(1, LANES)
        x1, meta, gate, cnt = _out_ln_route(o_a, o_b, x_parts, w_out.astype(BF16), ln1_g[l].reshape(1, d),
                                            ln1_b[l].reshape(1, d), wr, br, alpha)

        dest, block_e, n_valid, nb = _moe_plan(meta[:, :TOP_K], meta[:, TOP_K:2 * TOP_K],
                                               cnt[0, :N_EXPERTS].astype(jnp.int32), MOE_TB)
        dest2d = dest.reshape(t // ROW_TM, 1, TOP_K * ROW_TM)
        xs = _dispatch(dest2d, x1, jnp.zeros((nb * MOE_TB, d), F32))
        y = _experts(block_e, n_valid, xs, moe_w1, moe_w3, moe_w2, l)
        ln2 = (ln2_g[l].reshape(1, d), ln2_b[l].reshape(1, d))
        if l + 1 < depth:
            x, xb = _combine_ln(dest2d, gate, x1, *ln2, y, alpha, 0, t, True)
            x_parts = [x]
        else:
            outs = [_combine_ln(dest2d, gate, x1, *ln2, y, alpha, r0, b * s, False)[0]
                    for r0, b, s in segs]

    return (outs[0].reshape(x_prompt.shape), outs[1].reshape(x_sample.shape))
```

```python
import functools
import math

import jax
import jax.numpy as jnp
from jax import lax
from jax.experimental import pallas as pl
from jax.experimental.pallas import tpu as pltpu

F32 = jnp.float32
BF16 = jnp.bfloat16

GRID_W = 64
HEAD_DIM = 128
ROPE_THETA = 500000.0
DIFF_HEADS = 8
DIFF_QK_DIM = HEAD_DIM // 2
DIFF_ROT = DIFF_QK_DIM // 4
NA_HEADS = 8
NA_KH = 8
NA_KW = 16
MLA_HEADS = 8
MLA_Q_RANK = 512
MLA_KV_RANK = 256
MLA_NOPE = 128
MLA_ROPE = 64
MLA_V = 128
MLA_THETA = 10000.0
GQA_HEADS = 8
GQA_KV_HEADS = 2
AXIAL_THETA = 10000.0
N_GROUPS = 4
EXPERTS_PER_GROUP = 8
N_EXPERTS = N_GROUPS * EXPERTS_PER_GROUP
TOP_K = 2
LN_EPS = 1e-5
RMS_EPS = 1e-6
SUBLN_EPS = 1e-5

LANES = 128
V7X_VMEM_BYTES = 64 * 1024 * 1024
VMEM_LIMIT = V7X_VMEM_BYTES * 7 // 8

LOG2E = math.log2(math.e)
NEG_BIG = -1e30

PROJ_TM = 512
PROJ_TN = 1024
ROW_TM = 256
FLASH_ROWS = 1024
FLASH_TK = 1024
FLASH_MAX_TILES = 16
NA_ROWS = 8
MOE_TB = 256
DMA_UNROLL = 8

def _cparams(sem):
    return pltpu.CompilerParams(dimension_semantics=sem, vmem_limit_bytes=VMEM_LIMIT)


def _rope_tables(pos, lane_freq, lane_kind):
    if pos.ndim == 1:
        pos = pos[:, None]
    ang = pos * lane_freq[None, :]
    cos, sin = jnp.cos(ang), jnp.sin(ang)
    kind = lane_kind[None, :]
    c = jnp.where(kind == 2, 1.0, cos)
    s1 = jnp.where(kind == 0, -sin, 0.0)
    s2 = jnp.where(kind == 1, sin, 0.0)
    return jnp.stack([c, s1, s2]).astype(F32)


def _inv_freq(dim, theta):
    return theta ** (-jnp.arange(0, dim, 2, dtype=F32) / dim)


def _diff_tables(pos):
    lane = jnp.arange(LANES)
    within = lane % DIFF_QK_DIM
    half = DIFF_ROT // 2
    inv = _inv_freq(DIFF_ROT, ROPE_THETA)
    freq = inv[within % half]
    kind = jnp.where(within < half, 0, jnp.where(within < DIFF_ROT, 1, 2))
    return _rope_tables(pos, freq, kind)


def _mla_tables(pos):
    lane = jnp.arange(LANES)
    half = MLA_ROPE // 2
    inv = _inv_freq(MLA_ROPE, MLA_THETA)
    freq = inv[lane % half]
    kind = jnp.where(lane < half, 0, jnp.where(lane < MLA_ROPE, 1, 2))
    return _rope_tables(pos, freq, kind)


def _axial_tables(pos):
    lane = jnp.arange(LANES)
    hd = HEAD_DIM // 2
    half = hd // 2
    inv = _inv_freq(hd, AXIAL_THETA)
    freq = inv[lane % half]
    kind = jnp.where((lane % hd) < half, 0, 1)
    p = jnp.where(lane[None, :] < hd, (pos // GRID_W)[:, None], (pos % GRID_W)[:, None])
    return _rope_tables(p.astype(F32), freq, kind)


def _apply_rope(x, c, s1, s2, half):
    return x * c + pltpu.roll(x, LANES - half, 1) * s1 + pltpu.roll(x, half, 1) * s2


def _even_proj_kernel(x_ref, w_ref, tab_ref, o_ref, *, half):
    j = pl.program_id(0)
    acc = jnp.dot(x_ref[...], w_ref[...], preferred_element_type=F32)

    @pl.when(j < 2)
    def _():
        c, s1, s2 = tab_ref[0, 0], tab_ref[0, 1], tab_ref[0, 2]
        for ci in range(acc.shape[1] // LANES):
            sl = slice(ci * LANES, (ci + 1) * LANES)
            o_ref[:, sl] = _apply_rope(acc[:, sl], c, s1, s2, half).astype(o_ref.dtype)

    @pl.when(j >= 2)
    def _():
        o_ref[...] = acc.astype(o_ref.dtype)


def _even_proj(xb, w, tabs):
    t, d = xb.shape
    n = w.shape[1]
    tm, tn = PROJ_TM, PROJ_TN
    assert t % tm == 0 and n % tn == 0 and DIFF_HEADS * HEAD_DIM == tn
    return pl.pallas_call(
        functools.partial(_even_proj_kernel, half=DIFF_ROT // 2),
        grid=(n // tn, t // tm),
        in_specs=[
            pl.BlockSpec((tm, d), lambda j, i: (i, 0)),
            pl.BlockSpec((d, tn), lambda j, i: (0, j)),
            pl.BlockSpec((1, 3, tm, LANES), lambda j, i: (jnp.minimum(j, 1), 0, i, 0)),
        ],
        out_specs=pl.BlockSpec((tm, tn), lambda j, i: (i, j)),
        out_shape=jax.ShapeDtypeStruct((t, n), BF16),
        compiler_params=_cparams(("arbitrary", "arbitrary")),
        name="even_proj",
    )(xb, w, tabs)


def _plain_proj_kernel(x_ref, w_ref, o_ref):
    o_ref[...] = jnp.dot(x_ref[...], w_ref[...], preferred_element_type=F32).astype(o_ref.dtype)


def _plain_proj(xb, w):
    t, d = xb.shape
    n = w.shape[1]
    tm = PROJ_TM
    assert t % tm == 0 and n % LANES == 0
    return pl.pallas_call(
        _plain_proj_kernel,
        grid=(t // tm,),
        in_specs=[pl.BlockSpec((tm, d), lambda i: (i, 0)),
                  pl.BlockSpec((d, n), lambda i: (0, 0))],
        out_specs=pl.BlockSpec((tm, n), lambda i: (i, 0)),
        out_shape=jax.ShapeDtypeStruct((t, n), BF16),
        compiler_params=_cparams(("arbitrary",)),
        name="odd_proj",
    )(xb, w)


def _flash_scratch(rows, dqk, dv, tk, seq):
    return [pltpu.VMEM((rows, dqk), BF16),
            pltpu.VMEM((seq, dv + LANES), BF16),
            pltpu.VMEM((2, rows, tk), F32),
            pltpu.VMEM((rows, LANES), F32),
            pltpu.VMEM((rows, dv + LANES), F32)]


def _flash_loop(first, q_ref, k_ref, v_ref, vaug_ref, s_ref, m_ref, acc_ref, tk):
    n = k_ref.shape[0] // tk
    assert n <= FLASH_MAX_TILES
    nc = tk // LANES
    dv = v_ref.shape[1]

    @pl.when(first)
    def _():
        lane = lax.broadcasted_iota(jnp.int32, (v_ref.shape[0], LANES), 1)
        vaug_ref[:, 0:dv] = v_ref[...]
        vaug_ref[:, dv:] = jnp.where(lane == 0, 1.0, 0.0).astype(BF16)

    def scores(j):
        s_ref[j % 2] = lax.dot_general(q_ref[...], k_ref[j * tk:(j + 1) * tk, :],
                                       (((1,), (1,)), ((), ())), preferred_element_type=F32)

    def softmax_pv(j):
        slot = j % 2
        cmax = s_ref[slot, :, 0:LANES]
        for c in range(1, nc):
            cmax = jnp.maximum(cmax, s_ref[slot, :, c * LANES:(c + 1) * LANES])
        m_prev = m_ref[...]
        m_new = jnp.maximum(m_prev, jnp.max(cmax, axis=1, keepdims=True))
        a = jnp.exp2(m_prev - m_new)
        p = jnp.concatenate(
            [jnp.exp2(s_ref[slot, :, c * LANES:(c + 1) * LANES] - m_new).astype(BF16)
             for c in range(nc)], axis=1)
        pv = jnp.dot(p, vaug_ref[j * tk:(j + 1) * tk, :], preferred_element_type=F32)
        acc_ref[...] = jnp.concatenate([a, a], axis=1) * acc_ref[...] + pv
        m_ref[...] = m_new

    m_ref[...] = jnp.full(m_ref.shape, -jnp.inf, F32)
    acc_ref[...] = jnp.zeros(acc_ref.shape, F32)
    scores(0)
    for j in range(n):
        if j + 1 < n:
            scores(j + 1)
        softmax_pv(j)
    return acc_ref[:, 0:dv], acc_ref[:, dv:dv + 1]


def _segment_call(kernel_fn, *, grid, in_specs, args, out_spec, out_cols, rows, scratch, name):
    return pl.pallas_call(
        kernel_fn,
        grid=grid,
        in_specs=in_specs,
        out_specs=out_spec,
        out_shape=jax.ShapeDtypeStruct((rows, out_cols), BF16),
        scratch_shapes=scratch,
        compiler_params=_cparams(("arbitrary",) * len(grid)),
        name=name,
    )(*args)


def _flash_tk(seq):
    return min(FLASH_TK, seq // 2)


def _diff_attn_kernel(lam_ref, g_ref, q_ref, k_ref, v_ref, o_ref, q_scr, *scr, tk, lambda_init):
    tq = q_ref.shape[0]
    q = q_ref[...]
    lane = lax.broadcasted_iota(jnp.int32, q.shape, 1)
    zero = jnp.zeros_like(q)
    q_scr[0:tq] = jnp.where(lane < DIFF_QK_DIM, q, zero)
    q_scr[tq:2 * tq] = jnp.where(lane >= DIFF_QK_DIM, q, zero)
    acc, l = _flash_loop(pl.program_id(2) == 0, q_scr, k_ref, v_ref, *scr, tk)
    o = acc / l
    lv = lam_ref[...].astype(F32)
    lam = (jnp.exp(jnp.sum(lv[0:1] * lv[1:2], axis=1, keepdims=True))
           - jnp.exp(jnp.sum(lv[2:3] * lv[3:4], axis=1, keepdims=True)) + lambda_init)
    o = o[:tq] - lam * o[tq:]
    y = o * lax.rsqrt(jnp.mean(o * o, axis=1, keepdims=True) + SUBLN_EPS)
    o_ref[...] = (y * g_ref[...] * (1.0 - lambda_init)).astype(o_ref.dtype)


def _diff_attention(h, lam_vec, subln_g, row0, batch, seq, lambda_init):
    tq = FLASH_ROWS // 2
    tk = _flash_tk(seq)
    assert seq % tq == 0 and seq % tk == 0 and row0 % seq == 0
    nq = seq // tq
    hcols = DIFF_HEADS
    qrow = lambda b, hh, i: (row0 // tq + b * nq + i, hh)
    return _segment_call(
        functools.partial(_diff_attn_kernel, tk=tk, lambda_init=lambda_init),
        grid=(batch, DIFF_HEADS, nq),
        in_specs=[
            pl.BlockSpec((4, DIFF_QK_DIM), lambda b, hh, i: (0, 0)),
            pl.BlockSpec((1, HEAD_DIM), lambda b, hh, i: (0, 0)),
            pl.BlockSpec((tq, HEAD_DIM), qrow),
            pl.BlockSpec((seq, HEAD_DIM), lambda b, hh, i: (row0 // seq + b, hcols + hh)),
            pl.BlockSpec((seq, HEAD_DIM), lambda b, hh, i: (row0 // seq + b, 2 * hcols + hh)),
        ],
        args=[lam_vec, subln_g.reshape(1, HEAD_DIM), h, h, h],
        out_spec=pl.BlockSpec((tq, HEAD_DIM), lambda b, hh, i: (b * nq + i, hh)),
        out_cols=DIFF_HEADS * HEAD_DIM, rows=batch * seq,
        scratch=_flash_scratch(2 * tq, HEAD_DIM, HEAD_DIM, tk, seq), name="diff_attn")


def _mla_attn_kernel(q_ref, k_ref, v_ref, o_ref, q_scr, *scr, tk):
    del q_scr
    acc, l = _flash_loop(pl.program_id(2) == 0, q_ref, k_ref, v_ref, *scr, tk)
    o_ref[...] = (acc / l).astype(o_ref.dtype)


def _mla_attention(qm, km, vm, row0, batch, seq):
    tq = FLASH_ROWS
    tk = _flash_tk(seq)
    assert seq % tq == 0 and seq % tk == 0 and row0 % seq == 0
    nq = seq // tq
    dqk = 2 * LANES
    qrow = lambda b, hh, i: (row0 // tq + b * nq + i, hh)
    return _segment_call(
        functools.partial(_mla_attn_kernel, tk=tk),
        grid=(batch, MLA_HEADS, nq),
        in_specs=[
            pl.BlockSpec((tq, dqk), qrow),
            pl.BlockSpec((seq, dqk), lambda b, hh, i: (row0 // seq + b, hh)),
            pl.BlockSpec((seq, MLA_V), lambda b, hh, i: (row0 // seq + b, hh)),
        ],
        args=[qm, km, vm],
        out_spec=pl.BlockSpec((tq, MLA_V), lambda b, hh, i: (b * nq + i, hh)),
        out_cols=MLA_HEADS * MLA_V, rows=batch * seq,
        scratch=_flash_scratch(tq, 8, MLA_V, tk, seq), name="mla_attn")


def _gqa_attn_kernel(q_ref, k_ref, v_ref, o_ref, q_scr, *scr, tk, group):
    tq = q_ref.shape[0]
    for g in range(group):
        q_scr[g * tq:(g + 1) * tq] = q_ref[:, g * HEAD_DIM:(g + 1) * HEAD_DIM]
    acc, l = _flash_loop(pl.program_id(2) == 0, q_scr, k_ref, v_ref, *scr, tk)
    o = acc / l
    for g in range(group):
        o_ref[:, g * HEAD_DIM:(g + 1) * HEAD_DIM] = o[g * tq:(g + 1) * tq].astype(o_ref.dtype)


def _gqa_attention(qg, kg, h, v_col0, row0, batch, seq):
    group = GQA_HEADS // GQA_KV_HEADS
    tq = FLASH_ROWS // group
    tk = _flash_tk(seq)
    assert seq % tq == 0 and seq % tk == 0 and row0 % seq == 0
    nq = seq // tq
    qrow = lambda b, n, i: (row0 // tq + b * nq + i, n)
    return _segment_call(
        functools.partial(_gqa_attn_kernel, tk=tk, group=group),
        grid=(batch, GQA_KV_HEADS, nq),
        in_specs=[
            pl.BlockSpec((tq, group * HEAD_DIM), qrow),
            pl.BlockSpec((seq, HEAD_DIM), lambda b, n, i: (row0 // seq + b, n)),
            pl.BlockSpec((seq, HEAD_DIM), lambda b, n, i: (row0 // seq + b, v_col0 + n)),
        ],
        args=[qg, kg, h],
        out_spec=pl.BlockSpec((tq, group * HEAD_DIM), lambda b, n, i: (b * nq + i, n)),
        out_cols=GQA_HEADS * HEAD_DIM, rows=batch * seq,
        scratch=_flash_scratch(group * tq, HEAD_DIM, HEAD_DIM, tk, seq), name="gqa_attn")


def _na_bias_table(rpb):
    cols = jnp.arange(GRID_W)
    start = jnp.clip(cols - NA_KW // 2, 0, GRID_W - NA_KW)
    kc = cols[None, :]
    valid = (kc >= start[:, None]) & (kc < start[:, None] + NA_KW)
    dc = jnp.clip(kc - cols[:, None] + (NA_KW - 1), 0, 2 * NA_KW - 2)
    tab = rpb.astype(F32)[:, :, dc]
    tab = jnp.where(valid[None, None], tab, NEG_BIG)
    return jnp.concatenate([tab[:, :-1], tab[:, 1:]], axis=-1) * LOG2E


def _na_kernel(q_ref, k0_ref, k1_ref, k2_ref, k3_ref, v0_ref, v1_ref, v2_ref, v3_ref, bias_ref, o_ref,
               *, rows, scale):
    i = pl.program_id(1)
    k_refs = (k0_ref, k1_ref, k2_ref, k3_ref)
    v_refs = (v0_ref, v1_ref, v2_ref, v3_ref)
    key_row0 = jnp.clip(i * NA_ROWS - NA_KH // 2, 0, rows - 2 * NA_ROWS)
    d = i * NA_ROWS - key_row0
    n_pairs = NA_ROWS

    lane = lax.broadcasted_iota(jnp.int32, (8, LANES), 1)
    row_mask = []
    for j in range(NA_ROWS):
        first = jnp.clip(j + d - NA_KH // 2, 0, 2 * NA_ROWS - NA_KH)
        per_pair = []
        for m in range(n_pairs):
            key_row = 2 * m + jnp.where(lane >= GRID_W, 1, 0)
            madd = jnp.where(key_row >= first, jnp.where(key_row < first + NA_KH, 0.0, NEG_BIG), NEG_BIG)
            per_pair.append(jnp.concatenate([madd] * (GRID_W // 8), axis=0))
        row_mask.append(per_pair)

    for hh in range(NA_HEADS):
        cs = slice(hh * HEAD_DIM, (hh + 1) * HEAD_DIM)
        q = (q_ref[:, cs].astype(F32) * scale).astype(BF16)
        s = [lax.dot_general(q, kr[:, cs], (((1,), (1,)), ((), ())), preferred_element_type=F32)
             for kr in k_refs]
        cols = []
        for m in range(n_pairs):
            c, half = divmod(m, n_pairs // 4)
            pieces = []
            for j in range(NA_ROWS):
                dr = jnp.clip(2 * m - j - d + (NA_KH - 1), 0, 2 * NA_KH - 3)
                sb = s[c][j * GRID_W:(j + 1) * GRID_W, half * LANES:(half + 1) * LANES]
                pieces.append(sb + bias_ref[hh, dr] + row_mask[j][m])
            cols.append(jnp.concatenate(pieces, axis=0))
        cmax = cols[0]
        for col in cols[1:]:
            cmax = jnp.maximum(cmax, col)
        mx = jnp.max(cmax, axis=1, keepdims=True)
        ps = [jnp.exp2(col - mx) for col in cols]
        psum = ps[0]
        for p in ps[1:]:
            psum = psum + p
        l = jnp.sum(psum, axis=1, keepdims=True)
        per_q = n_pairs // 4
        o = None
        for c, vr in enumerate(v_refs):
            pc = jnp.concatenate([p.astype(BF16) for p in ps[c * per_q:(c + 1) * per_q]], axis=1)
            oc = jnp.dot(pc, vr[:, cs], preferred_element_type=F32)
            o = oc if o is None else o + oc
        o_ref[:, cs] = (o / l).astype(o_ref.dtype)


def _na_attention(h, bias_tab, col0, row0, batch, seq):
    rows = seq // GRID_W
    assert NA_ROWS == NA_KH and NA_ROWS % 4 == 0
    assert seq % GRID_W == 0 and rows % NA_ROWS == 0 and rows >= 2 * NA_ROWS
    nb = rows // NA_ROWS
    blk = NA_ROWS * GRID_W
    qtr = blk // 2
    width = NA_HEADS * HEAD_DIM
    assert row0 % blk == 0
    qbase = row0 // blk
    kbase = row0 // qtr
    per_seq = seq // qtr

    def win(c, col):
        def index(b, i):
            first = jnp.clip(2 * i - 1, 0, per_seq - 4)
            return (kbase + b * per_seq + first + c, col)
        return pl.BlockSpec((qtr, width), index)

    qspec = lambda col: pl.BlockSpec((blk, width), lambda b, i: (qbase + b * nb + i, col))
    return _segment_call(
        functools.partial(_na_kernel, rows=rows, scale=HEAD_DIM ** -0.5 * LOG2E),
        grid=(batch, nb),
        in_specs=[qspec(col0)] + [win(c, col0 + 1) for c in range(4)] + [win(c, col0 + 2) for c in range(4)]
                 + [pl.BlockSpec(bias_tab.shape, lambda b, i: (0, 0, 0, 0))],
        args=[h] * 9 + [bias_tab],
        out_spec=pl.BlockSpec((blk, width), lambda b, i: (b * nb + i, 0)),
        out_cols=width, rows=batch * seq,
        scratch=[], name="na_attn")


def _rms(x, g, eps):
    return x * lax.rsqrt(jnp.mean(x * x, axis=1, keepdims=True) + eps) * g


def _odd_prep_kernel(h_ref, wq_ref, wk_ref, wv_ref, qn_ref, kvn_ref, gqn_ref, gkn_ref,
                     mtab_ref, atab_ref, qm_ref, km_ref, vm_ref, qg_ref, kg_ref, *, cols):
    c_qc, c_kvc, c_gq, c_gk, c_kpe = cols
    mla_scale = (MLA_NOPE + MLA_ROPE) ** -0.5 * LOG2E
    gqa_scale = HEAD_DIM ** -0.5 * LOG2E
    mc, ms1, ms2 = mtab_ref[0], mtab_ref[1], mtab_ref[2]
    ac, as1, as2 = atab_ref[0], atab_ref[1], atab_ref[2]
    mhalf = MLA_ROPE // 2
    ahalf = HEAD_DIM // 4

    qc = _rms(h_ref[:, c_qc:c_qc + MLA_Q_RANK].astype(F32), qn_ref[...], RMS_EPS)
    q = jnp.dot(qc.astype(BF16), wq_ref[...], preferred_element_type=F32)
    for hh in range(MLA_HEADS):
        lo = hh * 2 * LANES
        qm_ref[:, lo:lo + LANES] = (q[:, lo:lo + LANES] * mla_scale).astype(qm_ref.dtype)
        qr = _apply_rope(q[:, lo + LANES:lo + 2 * LANES], mc, ms1, ms2, mhalf)
        qm_ref[:, lo + LANES:lo + 2 * LANES] = (qr * mla_scale).astype(qm_ref.dtype)

    kvc = _rms(h_ref[:, c_kvc:c_kvc + MLA_KV_RANK].astype(F32), kvn_ref[...], RMS_EPS).astype(BF16)
    kn = jnp.dot(kvc, wk_ref[...], preferred_element_type=F32)
    vm_ref[...] = jnp.dot(kvc, wv_ref[...], preferred_element_type=F32).astype(vm_ref.dtype)
    kpe = _apply_rope(h_ref[:, c_kpe:c_kpe + LANES].astype(F32), mc, ms1, ms2, mhalf)
    kpe = kpe.astype(km_ref.dtype)
    for hh in range(MLA_HEADS):
        lo = hh * 2 * LANES
        km_ref[:, lo:lo + LANES] = kn[:, hh * LANES:(hh + 1) * LANES].astype(km_ref.dtype)
        km_ref[:, lo + LANES:lo + 2 * LANES] = kpe

    for hh in range(GQA_HEADS):
        sl = slice(c_gq + hh * HEAD_DIM, c_gq + (hh + 1) * HEAD_DIM)
        x = _rms(h_ref[:, sl].astype(F32), gqn_ref[...], RMS_EPS)
        x = _apply_rope(x, ac, as1, as2, ahalf) * gqa_scale
        qg_ref[:, hh * HEAD_DIM:(hh + 1) * HEAD_DIM] = x.astype(qg_ref.dtype)
    for hh in range(GQA_KV_HEADS):
        sl = slice(c_gk + hh * HEAD_DIM, c_gk + (hh + 1) * HEAD_DIM)
        x = _rms(h_ref[:, sl].astype(F32), gkn_ref[...], RMS_EPS)
        x = _apply_rope(x, ac, as1, as2, ahalf)
        kg_ref[:, hh * HEAD_DIM:(hh + 1) * HEAD_DIM] = x.astype(kg_ref.dtype)


def _odd_prep(h, wq, wk, wv, qn, kvn, gqn, gkn, mtab, atab, cols):
    t, n = h.shape
    tm = ROW_TM
    assert t % tm == 0
    row = lambda i: (i, 0)
    full = lambda i: (0, 0)
    tab = lambda i: (0, i, 0)
    widths = (MLA_HEADS * 2 * LANES, MLA_HEADS * 2 * LANES, MLA_HEADS * MLA_V,
              GQA_HEADS * HEAD_DIM, GQA_KV_HEADS * HEAD_DIM)
    return pl.pallas_call(
        functools.partial(_odd_prep_kernel, cols=cols),
        grid=(t // tm,),
        in_specs=[pl.BlockSpec((tm, n), row),
                  pl.BlockSpec(wq.shape, full), pl.BlockSpec(wk.shape, full), pl.BlockSpec(wv.shape, full),
                  pl.BlockSpec(qn.shape, full), pl.BlockSpec(kvn.shape, full),
                  pl.BlockSpec(gqn.shape, full), pl.BlockSpec(gkn.shape, full),
                  pl.BlockSpec((3, tm, LANES), tab), pl.BlockSpec((3, tm, LANES), tab)],
        out_specs=[pl.BlockSpec((tm, w), row) for w in widths],
        out_shape=[jax.ShapeDtypeStruct((t, w), BF16) for w in widths],
        compiler_params=_cparams(("arbitrary",)),
        name="odd_prep",
    )(h, wq, wk, wv, qn, kvn, gqn, gkn, mtab, atab)


def _layer_norm(y, g, b):
    mu = jnp.mean(y, axis=1, keepdims=True)
    yc = y - mu
    var = jnp.mean(yc * yc, axis=1, keepdims=True)
    return yc * lax.rsqrt(var + LN_EPS) * g + b


def _route(logits):
    lane = lax.broadcasted_iota(jnp.int32, logits.shape, 1)
    lanef = lane.astype(F32)
    big = float(LANES)
    gmask = lane < N_GROUPS
    gl = jnp.where(gmask, logits, -jnp.inf)
    gmax = jnp.max(gl, axis=1, keepdims=True)
    gsel = jnp.min(jnp.where(gl == gmax, lanef, big), axis=1, keepdims=True)
    gden = jnp.sum(jnp.where(gmask, jnp.exp(gl - gmax), 0.0), axis=1, keepdims=True)
    gprob = 1.0 / gden
    lo = N_GROUPS + EXPERTS_PER_GROUP * gsel
    emask = (lanef >= lo) & (lanef < lo + EXPERTS_PER_GROUP)
    el = jnp.where(emask, logits, -jnp.inf)
    v1 = jnp.max(el, axis=1, keepdims=True)
    i1 = jnp.min(jnp.where(el == v1, lanef, big), axis=1, keepdims=True)
    el2 = jnp.where(lanef == i1, -jnp.inf, el)
    v2 = jnp.max(el2, axis=1, keepdims=True)
    i2 = jnp.min(jnp.where(el2 == v2, lanef, big), axis=1, keepdims=True)
    e = jnp.exp(v2 - v1)
    g1 = gprob / (1.0 + e)
    g2 = gprob * e / (1.0 + e)
    gate = jnp.where(lane == 0, g1, jnp.where(lane == 1, g2, 0.0))
    return i1 - N_GROUPS, i2 - N_GROUPS, gate


def _rank_in_expert(e1, e2, cnt_ref):
    tm = e1.shape[0]
    lanef = lax.broadcasted_iota(jnp.int32, (tm, LANES), 1).astype(F32)
    sel1 = lanef == e1
    sel2 = lanef == e2
    both = jnp.where(sel1, 1.0, jnp.where(sel2, 1.0, 0.0))
    row = lax.broadcasted_iota(jnp.int32, (tm, tm), 0)
    col = lax.broadcasted_iota(jnp.int32, (tm, tm), 1)
    ltri = jnp.where(col < row, 1.0, 0.0).astype(BF16)
    before = jnp.dot(ltri, both.astype(BF16), preferred_element_type=F32) + cnt_ref[...]
    r1 = jnp.sum(jnp.where(sel1, before, 0.0), axis=1, keepdims=True)
    r2 = jnp.sum(jnp.where(sel2, before, 0.0), axis=1, keepdims=True)
    cnt_ref[...] = cnt_ref[...] + jnp.sum(both, axis=0, keepdims=True)
    return r1, r2


def _pick_part(refs, ends, step):
    x = refs[-1][...]
    for ref, end in zip(refs[-2::-1], ends[-2::-1]):
        x = jnp.where(step < end, ref[...], x)
    return x


def _out_ln_route_kernel(*refs, alpha, ends):
    n1, n2, n3 = (len(e) for e in ends)
    a1_refs, a2_refs, x_refs = refs[:n1], refs[n1:n1 + n2], refs[n1 + n2:n1 + n2 + n3]
    w_ref, g_ref, b_ref, wr_ref, br_ref, x1_ref, meta_ref, gate_ref, cnt_ref = refs[n1 + n2 + n3:]
    step = pl.program_id(0)

    @pl.when(step == 0)
    def _():
        cnt_ref[...] = jnp.zeros(cnt_ref.shape, F32)

    a1 = _pick_part(a1_refs, ends[0], step)
    a2 = _pick_part(a2_refs, ends[1], step)
    x = _pick_part(x_refs, ends[2], step)
    k1 = a1.shape[1]
    mix = (jnp.dot(a1, w_ref[0:k1, :], preferred_element_type=F32)
           + jnp.dot(a2, w_ref[k1:, :], preferred_element_type=F32))
    x1 = _layer_norm(alpha * x + mix, g_ref[...], b_ref[...])
    x1_ref[...] = x1
    x_hi = x1.astype(BF16)
    x_mid = (x1 - x_hi.astype(F32)).astype(BF16)
    t = (jnp.dot(x_hi, wr_ref[...], preferred_element_type=F32)
         + jnp.dot(x_mid, wr_ref[...], preferred_element_type=F32))
    logits = t[:, :LANES] + t[:, LANES:] + br_ref[...]
    e1, e2, gate = _route(logits)
    r1, r2 = _rank_in_expert(e1, e2, cnt_ref)
    lane = lax.broadcasted_iota(jnp.int32, gate.shape, 1)
    meta = jnp.where(lane == 0, e1, jnp.where(lane == 1, e2,
                                              jnp.where(lane == 2, r1, jnp.where(lane == 3, r2, 0.0))))
    meta_ref[...] = meta.astype(jnp.int32)
    gate_ref[...] = gate


def _part_specs(parts, tm):
    specs, ends, start = [], [], 0
    for p in parts:
        assert p.shape[0] % tm == 0
        n = p.shape[0] // tm
        specs.append(pl.BlockSpec((tm, p.shape[1]), functools.partial(
            lambda i, start, n: (jnp.clip(i - start, 0, n - 1), 0), start=start, n=n)))
        start += n
        ends.append(start)
    return specs, tuple(ends)


def _out_ln_route(a1_parts, a2_parts, x_parts, w, g, b, wr, br, alpha):
    d = x_parts[0].shape[1]
    t = sum(xp.shape[0] for xp in x_parts)
    tm = ROW_TM
    row = lambda i: (i, 0)
    full = lambda i: (0, 0)
    spec_lists, ends = zip(*(_part_specs(parts, tm) for parts in (a1_parts, a2_parts, x_parts)))
    assert all(e[-1] == t // tm for e in ends)
    return pl.pallas_call(
        functools.partial(_out_ln_route_kernel, alpha=alpha, ends=ends),
        grid=(t // tm,),
        in_specs=[*spec_lists[0], *spec_lists[1], *spec_lists[2],
                  pl.BlockSpec(w.shape, full),
                  pl.BlockSpec((1, d), full), pl.BlockSpec((1, d), full),
                  pl.BlockSpec(wr.shape, full), pl.BlockSpec((1, LANES), full)],
        out_specs=[pl.BlockSpec((tm, d), row), pl.BlockSpec((tm, LANES), row),
                   pl.BlockSpec((tm, LANES), row), pl.BlockSpec((1, LANES), full)],
        out_shape=[jax.ShapeDtypeStruct((t, d), F32), jax.ShapeDtypeStruct((t, LANES), jnp.int32),
                   jax.ShapeDtypeStruct((t, LANES), F32), jax.ShapeDtypeStruct((1, LANES), F32)],
        compiler_params=_cparams(("arbitrary",)),
        name="out_ln_route",
    )(*a1_parts, *a2_parts, *x_parts, w, g, b, wr, br)


def _row_copy(src_ref, src_row, dst_ref, dst_row, sem):
    return pltpu.make_async_copy(src_ref.at[pl.ds(src_row, 1)], dst_ref.at[pl.ds(dst_row, 1)], sem)


def _dispatch_kernel(dest_ref, x_ref, xs_in_ref, xs_ref, sem):
    del xs_in_ref
    tm = x_ref.shape[0]

    def start(r, _):
        for k in range(TOP_K):
            _row_copy(x_ref, r, xs_ref, dest_ref[0, 0, TOP_K * r + k], sem).start()
        return 0

    def wait(r, _):
        for k in range(TOP_K):
            _row_copy(x_ref, r, xs_ref, dest_ref[0, 0, TOP_K * r + k], sem).wait()
        return 0

    lax.fori_loop(0, tm, start, 0, unroll=DMA_UNROLL)
    lax.fori_loop(0, tm, wait, 0, unroll=DMA_UNROLL)


def _dispatch(dest2d, x, xs_init):
    t, d = x.shape
    tm = ROW_TM
    return pl.pallas_call(
        _dispatch_kernel,
        grid=(t // tm,),
        in_specs=[pl.BlockSpec((1, 1, TOP_K * tm), lambda i: (i, 0, 0), memory_space=pltpu.SMEM),
                  pl.BlockSpec((tm, d), lambda i: (i, 0)),
                  pl.BlockSpec(memory_space=pl.ANY)],
        out_specs=pl.BlockSpec(memory_space=pl.ANY),
        out_shape=jax.ShapeDtypeStruct(xs_init.shape, xs_init.dtype),
        scratch_shapes=[pltpu.SemaphoreType.DMA(())],
        input_output_aliases={2: 0},
        compiler_params=_cparams(("arbitrary",)),
        name="moe_dispatch",
    )(dest2d, x, xs_init)


def _expert_kernel(be_ref, nv_ref, xs_ref, w1_ref, w3_ref, w2_ref, y_ref, w1b, w3b, w2b):
    b = pl.program_id(0)
    valid = b < nv_ref[0]
    new_expert = jnp.logical_or(b == 0, be_ref[b] != be_ref[jnp.maximum(b - 1, 0)])

    @pl.when(jnp.logical_and(valid, new_expert))
    def _():
        w1b[...] = w1_ref[0, 0].astype(BF16)
        w3b[...] = w3_ref[0, 0].astype(BF16)
        w2b[...] = w2_ref[0, 0].astype(BF16)

    @pl.when(valid)
    def _():
        x = xs_ref[...].astype(BF16)
        h1 = jnp.dot(x, w1b[...], preferred_element_type=F32)
        h3 = jnp.dot(x, w3b[...], preferred_element_type=F32)
        hm = (h1 / (1.0 + jnp.exp(-h1)) * h3).astype(BF16)
        y_ref[...] = jnp.dot(hm, w2b[...], preferred_element_type=F32).astype(y_ref.dtype)

    @pl.when(jnp.logical_not(valid))
    def _():
        y_ref[...] = jnp.zeros_like(y_ref)


def _experts(block_e, n_valid, xs, w1, w3, w2, layer):
    p, d = xs.shape
    tb = MOE_TB
    nb = p // tb
    de = w1.shape[3]
    last = lambda b, nv: jnp.minimum(b, nv[0] - 1)
    wmap = lambda b, be, nv: (layer, be[last(b, nv)], 0, 0)
    return pl.pallas_call(
        _expert_kernel,
        grid_spec=pltpu.PrefetchScalarGridSpec(
            num_scalar_prefetch=2,
            grid=(nb,),
            in_specs=[pl.BlockSpec((tb, d), lambda b, be, nv: (last(b, nv), 0)),
                      pl.BlockSpec((1, 1, d, de), wmap), pl.BlockSpec((1, 1, d, de), wmap),
                      pl.BlockSpec((1, 1, de, d), wmap)],
            out_specs=pl.BlockSpec((tb, d), lambda b, be, nv: (b, 0)),
            scratch_shapes=[pltpu.VMEM((d, de), BF16), pltpu.VMEM((d, de), BF16),
                            pltpu.VMEM((de, d), BF16)],
        ),
        out_shape=jax.ShapeDtypeStruct((p, d), F32),
        compiler_params=_cparams(("arbitrary",)),
        name="moe_experts",
    )(block_e, n_valid, xs, w1, w3, w2)


def _combine_ln_kernel(dest_ref, next_dest_ref, gate_ref, x_ref, g_ref, b_ref, y_ref, o_ref, *rest, alpha):
    *maybe_ob_ref, buf, sems = rest
    tm = x_ref.shape[0]
    step = pl.program_id(0)
    last = pl.num_programs(0) - 1
    slot = step % 2
    other = 1 - slot

    def gather(idx_ref, r, into):
        return [_row_copy(y_ref, idx_ref[0, 0, TOP_K * r + k], buf.at[into, k], r, sems.at[into])
                for k in range(TOP_K)]

    def start_rows(idx_ref, into, lo, n):
        def body(r, _):
            for cp in gather(idx_ref, lo + r, into):
                cp.start()
            return 0
        lax.fori_loop(0, n, body, 0, unroll=DMA_UNROLL)

    def wait_rows(idx_ref, into):
        def body(r, _):
            for cp in gather(idx_ref, r, into):
                cp.wait()
            return 0
        lax.fori_loop(0, tm, body, 0, unroll=DMA_UNROLL)

    @pl.when(step == 0)
    def _():
        start_rows(dest_ref, slot, 0, tm)

    wait_rows(dest_ref, slot)

    start_rows(next_dest_ref, other, 0, tm)
    gate = gate_ref[...]
    ff = buf[slot, 0] * gate[:, 0:1] + buf[slot, 1] * gate[:, 1:2]
    x2 = _layer_norm(alpha * x_ref[...] + ff, g_ref[...], b_ref[...])
    o_ref[...] = x2
    for ob_ref in maybe_ob_ref:
        ob_ref[...] = x2.astype(ob_ref.dtype)

    @pl.when(step == last)
    def _():
        wait_rows(next_dest_ref, other)


def _combine_ln(dest2d, gate, x, g, b, y, alpha, row0, nrows, with_bf16):
    d = x.shape[1]
    tm = ROW_TM
    assert row0 % tm == 0 and nrows % tm == 0
    base = row0 // tm
    row_in = lambda i: (base + i, 0)
    row_out = lambda i: (i, 0)
    full = lambda i: (0, 0)
    n_out = 2 if with_bf16 else 1
    n_steps = nrows // tm
    idx_spec = lambda f: pl.BlockSpec((1, 1, TOP_K * tm), f, memory_space=pltpu.SMEM)
    return pl.pallas_call(
        functools.partial(_combine_ln_kernel, alpha=alpha),
        grid=(n_steps,),
        in_specs=[idx_spec(lambda i: (base + i, 0, 0)),
                  idx_spec(lambda i: (base + jnp.minimum(i + 1, n_steps - 1), 0, 0)),
                  pl.BlockSpec((tm, LANES), row_in), pl.BlockSpec((tm, d), row_in),
                  pl.BlockSpec((1, d), full), pl.BlockSpec((1, d), full),
                  pl.BlockSpec(memory_space=pl.ANY)],
        out_specs=[pl.BlockSpec((tm, d), row_out)] * n_out,
        out_shape=[jax.ShapeDtypeStruct((nrows, d), F32), jax.ShapeDtypeStruct((nrows, d), BF16)][:n_out],
        scratch_shapes=[pltpu.VMEM((2, TOP_K, tm, d), F32), pltpu.SemaphoreType.DMA((2,))],
        compiler_params=_cparams(("arbitrary",)),
        name="moe_combine_ln",
    )(dest2d, dest2d, gate, x, g, b, y)


def _moe_plan(eid, rank, counts, tb):
    a = eid.shape[0] * TOP_K
    padded = (counts + tb - 1) // tb * tb
    pend = jnp.cumsum(padded)
    pstart = pend - padded
    dest = (jnp.take(pstart, eid) + rank).astype(jnp.int32)
    nb = -(-a // tb) + N_EXPERTS
    block_row0 = jnp.arange(nb, dtype=jnp.int32) * tb
    block_e = jnp.minimum(jnp.sum((pend[None, :] <= block_row0[:, None]).astype(jnp.int32), axis=1),
                          N_EXPERTS - 1).astype(jnp.int32)
    n_valid = (pend[-1] // tb).astype(jnp.int32).reshape(1)
    return dest, block_e, n_valid, nb


def _segments(groups):
    segs, row0 = [], 0
    for b, s in groups:
        segs.append((row0, b, s))
        row0 += b * s
    return segs


def kernel(x_prompt, x_sample, ev_w_in, ev_w_out, diff_lambda, diff_subln, na_rpb, od_w_in, od_w_out,
           mla_q_norm, mla_w_q_up, mla_kv_norm, mla_w_kv_up, gqa_q_norm, gqa_k_norm,
           ln1_g, ln1_b, ln2_g, ln2_b, moe_w_group, moe_b_group, moe_w_expert, moe_b_expert,
           moe_w1, moe_w3, moe_w2):
    d = x_prompt.shape[-1]
    depth = ln1_g.shape[0]
    alpha = (2 * depth) ** 0.25
    groups = [(x_prompt.shape[0], x_prompt.shape[1]), (x_sample.shape[0], x_sample.shape[1])]
    segs = _segments(groups)
    x_parts = [x_prompt.reshape(-1, d), x_sample.reshape(-1, d)]
    t = sum(xp.shape[0] for xp in x_parts)
    xb = jnp.concatenate([xp.astype(BF16) for xp in x_parts], axis=0)
    pos = jnp.concatenate([jnp.tile(jnp.arange(s, dtype=jnp.int32), b) for b, s in groups])
    posf = pos.astype(F32)

    for l in range(depth):
        i = l // 2
        if l % 2 == 0:
            tab = _diff_tables(posf)
            qscale = DIFF_QK_DIM ** -0.5 * LOG2E
            tabs = jnp.stack([tab * qscale, tab])
            h = _even_proj(xb, ev_w_in[i].astype(BF16), tabs)
            lambda_init = 0.8 - 0.6 * math.exp(-0.3 * l)
            bias_tab = _na_bias_table(na_rpb[i])
            o_a = [_diff_attention(h, diff_lambda[i], diff_subln[i], r0, b, s, lambda_init)
                   for r0, b, s in segs]
            o_b = [_na_attention(h, bias_tab, 3, r0, b, s) for r0, b, s in segs]
            w_out = ev_w_out[i]
        else:
            s1 = MLA_Q_RANK
            s2 = s1 + MLA_KV_RANK
            s3 = s2 + MLA_ROPE
            w_in = od_w_in[i]
            w_perm = jnp.concatenate(
                [w_in[:, :s2], w_in[:, s3:], w_in[:, s2:s3], jnp.zeros((d, LANES - MLA_ROPE), w_in.dtype)],
                axis=1).astype(BF16)
            c_gq = s2
            c_gk = c_gq + GQA_HEADS * HEAD_DIM
            c_gv = c_gk + GQA_KV_HEADS * HEAD_DIM
            c_kpe = c_gv + GQA_KV_HEADS * HEAD_DIM
            h = _plain_proj(xb, w_perm)
            wq = mla_w_q_up[i].reshape(MLA_Q_RANK, MLA_HEADS, MLA_NOPE + MLA_ROPE)
            wq = jnp.pad(wq, ((0, 0), (0, 0), (0, 2 * LANES - MLA_NOPE - MLA_ROPE)))
            wq = wq.reshape(MLA_Q_RANK, MLA_HEADS * 2 * LANES).astype(BF16)
            wkv = mla_w_kv_up[i].reshape(MLA_KV_RANK, MLA_HEADS, MLA_NOPE + MLA_V)
            wk = wkv[:, :, :MLA_NOPE].reshape(MLA_KV_RANK, MLA_HEADS * MLA_NOPE).astype(BF16)
            wv = wkv[:, :, MLA_NOPE:].reshape(MLA_KV_RANK, MLA_HEADS * MLA_V).astype(BF16)
            qm, km, vm, qg, kg = _odd_prep(
                h, wq, wk, wv, mla_q_norm[i].reshape(1, -1), mla_kv_norm[i].reshape(1, -1),
                gqa_q_norm[i].reshape(1, -1), gqa_k_norm[i].reshape(1, -1),
                _mla_tables(posf), _axial_tables(pos), (0, s1, c_gq, c_gk, c_kpe))
            o_a = [_mla_attention(qm, km, vm, r0, b, s) for r0, b, s in segs]
            o_b = [_gqa_attention(qg, kg, h, c_gv // HEAD_DIM, r0, b, s) for r0, b, s in segs]
            w_out = od_w_out[i]

        wr = jnp.concatenate([moe_w_group[l], moe_w_expert[l]], axis=1)
        wr = jnp.pad(wr, ((0, 0), (0, LANES - wr.shape[1])))
        wr_hi = wr.astype(BF16)
        wr = jnp.concatenate([wr_hi, (wr - wr_hi.astype(F32)).astype(BF16)], axis=1)
        br = jnp.concatenate([moe_b_group[l], moe_b_expert[l]])
        br = jnp.pad(br, (0, LANES - br.shape[0])).reshape(1, LANES)
        x1, meta, gate, cnt = _out_ln_route(o_a, o_b, x_parts, w_out.astype(BF16), ln1_g[l].reshape(1, d),
                                            ln1_b[l].reshape(1, d), wr, br, alpha)

        dest, block_e, n_valid, nb = _moe_plan(meta[:, :TOP_K], meta[:, TOP_K:2 * TOP_K],
                                               cnt[0, :N_EXPERTS].astype(jnp.int32), MOE_TB)
        dest2d = dest.reshape(t // ROW_TM, 1, TOP_K * ROW_TM)
        xs = _dispatch(dest2d, x1, jnp.zeros((nb * MOE_TB, d), F32))
        y = _experts(block_e, n_valid, xs, moe_w1, moe_w3, moe_w2, l)
        ln2 = (ln2_g[l].reshape(1, d), ln2_b[l].reshape(1, d))
        if l + 1 < depth:
            x, xb = _combine_ln(dest2d, gate, x1, *ln2, y, alpha, 0, t, True)
            x_parts = [x]
        else:
            outs = [_combine_ln(dest2d, gate, x1, *ln2, y, alpha, r0, b * s, False)[0]
                    for r0, b, s in segs]

    return (outs[0].reshape(x_prompt.shape), outs[1].reshape(x_sample.shape))
```

```python
import functools
import math

import jax
import jax.numpy as jnp
from jax import lax
from jax.experimental import pallas as pl
from jax.experimental.pallas import tpu as pltpu

F32 = jnp.float32
BF16 = jnp.bfloat16

GRID_W = 64
HEAD_DIM = 128
ROPE_THETA = 500000.0
DIFF_HEADS = 8
DIFF_QK_DIM = HEAD_DIM // 2
DIFF_ROT = DIFF_QK_DIM // 4
NA_HEADS = 8
NA_KH = 8
NA_KW = 16
MLA_HEADS = 8
MLA_Q_RANK = 512
MLA_KV_RANK = 256
MLA_NOPE = 128
MLA_ROPE = 64
MLA_V = 128
MLA_THETA = 10000.0
GQA_HEADS = 8
GQA_KV_HEADS = 2
AXIAL_THETA = 10000.0
N_GROUPS = 4
EXPERTS_PER_GROUP = 8
N_EXPERTS = N_GROUPS * EXPERTS_PER_GROUP
TOP_K = 2
LN_EPS = 1e-5
RMS_EPS = 1e-6
SUBLN_EPS = 1e-5

LANES = 128
V7X_VMEM_BYTES = 64 * 1024 * 1024
VMEM_LIMIT = V7X_VMEM_BYTES * 7 // 8

LOG2E = math.log2(math.e)
NEG_BIG = -1e30

PROJ_TM = 512
PROJ_TN = 1024
ROW_TM = 256
FLASH_ROWS = 1024
FLASH_TK = 1024
FLASH_MAX_TILES = 16
NA_ROWS = 8
MOE_TB = 256
DMA_UNROLL = 8

def _cparams(sem):
    return pltpu.CompilerParams(dimension_semantics=sem, vmem_limit_bytes=VMEM_LIMIT)


def _rope_tables(pos, lane_freq, lane_kind):
    if pos.ndim == 1:
        pos = pos[:, None]
    ang = pos * lane_freq[None, :]
    cos, sin = jnp.cos(ang), jnp.sin(ang)
    kind = lane_kind[None, :]
    c = jnp.where(kind == 2, 1.0, cos)
    s1 = jnp.where(kind == 0, -sin, 0.0)
    s2 = jnp.where(kind == 1, sin, 0.0)
    return jnp.stack([c, s1, s2]).astype(F32)


def _inv_freq(dim, theta):
    return theta ** (-jnp.arange(0, dim, 2, dtype=F32) / dim)


def _diff_tables(pos):
    lane = jnp.arange(LANES)
    within = lane % DIFF_QK_DIM
    half = DIFF_ROT // 2
    inv = _inv_freq(DIFF_ROT, ROPE_THETA)
    freq = inv[within % half]
    kind = jnp.where(within < half, 0, jnp.where(within < DIFF_ROT, 1, 2))
    return _rope_tables(pos, freq, kind)


def _mla_tables(pos):
    lane = jnp.arange(LANES)
    half = MLA_ROPE // 2
    inv = _inv_freq(MLA_ROPE, MLA_THETA)
    freq = inv[lane % half]
    kind = jnp.where(lane < half, 0, jnp.where(lane < MLA_ROPE, 1, 2))
    return _rope_tables(pos, freq, kind)


def _axial_tables(pos):
    lane = jnp.arange(LANES)
    hd = HEAD_DIM // 2
    half = hd // 2
    inv = _inv_freq(hd, AXIAL_THETA)
    freq = inv[lane % half]
    kind = jnp.where((lane % hd) < half, 0, 1)
    p = jnp.where(lane[None, :] < hd, (pos // GRID_W)[:, None], (pos % GRID_W)[:, None])
    return _rope_tables(p.astype(F32), freq, kind)


def _apply_rope(x, c, s1, s2, half):
    return x * c + pltpu.roll(x, LANES - half, 1) * s1 + pltpu.roll(x, half, 1) * s2


def _even_proj_kernel(x_ref, w_ref, tab_ref, o_ref, *, half):
    j = pl.program_id(0)
    acc = jnp.dot(x_ref[...], w_ref[...], preferred_element_type=F32)

    @pl.when(j < 2)
    def _():
        c, s1, s2 = tab_ref[0, 0], tab_ref[0, 1], tab_ref[0, 2]
        for ci in range(acc.shape[1] // LANES):
            sl = slice(ci * LANES, (ci + 1) * LANES)
            o_ref[:, sl] = _apply_rope(acc[:, sl], c, s1, s2, half).astype(o_ref.dtype)

    @pl.when(j >= 2)
    def _():
        o_ref[...] = acc.astype(o_ref.dtype)


def _even_proj(xb, w, tabs):
    t, d = xb.shape
    n = w.shape[1]
    tm, tn = PROJ_TM, PROJ_TN
    assert t % tm == 0 and n % tn == 0 and DIFF_HEADS * HEAD_DIM == tn
    return pl.pallas_call(
        functools.partial(_even_proj_kernel, half=DIFF_ROT // 2),
        grid=(n // tn, t // tm),
        in_specs=[
            pl.BlockSpec((tm, d), lambda j, i: (i, 0)),
            pl.BlockSpec((d, tn), lambda j, i: (0, j)),
            pl.BlockSpec((1, 3, tm, LANES), lambda j, i: (jnp.minimum(j, 1), 0, i, 0)),
        ],
        out_specs=pl.BlockSpec((tm, tn), lambda j, i: (i, j)),
        out_shape=jax.ShapeDtypeStruct((t, n), BF16),
        compiler_params=_cparams(("arbitrary", "arbitrary")),
        name="even_proj",
    )(xb, w, tabs)


def _plain_proj_kernel(x_ref, w_ref, o_ref):
    o_ref[...] = jnp.dot(x_ref[...], w_ref[...], preferred_element_type=F32).astype(o_ref.dtype)


def _plain_proj(xb, w):
    t, d = xb.shape
    n = w.shape[1]
    tm = PROJ_TM
    assert t % tm == 0 and n % LANES == 0
    return pl.pallas_call(
        _plain_proj_kernel,
        grid=(t // tm,),
        in_specs=[pl.BlockSpec((tm, d), lambda i: (i, 0)),
                  pl.BlockSpec((d, n), lambda i: (0, 0))],
        out_specs=pl.BlockSpec((tm, n), lambda i: (i, 0)),
        out_shape=jax.ShapeDtypeStruct((t, n), BF16),
        compiler_params=_cparams(("arbitrary",)),
        name="odd_proj",
    )(xb, w)


def _flash_scratch(rows, dqk, dv, tk, seq):
    return [pltpu.VMEM((rows, dqk), BF16),
            pltpu.VMEM((seq, dv + LANES), BF16),
            pltpu.VMEM((2, rows, tk), F32),
            pltpu.VMEM((rows, LANES), F32),
            pltpu.VMEM((rows, dv + LANES), F32)]


def _flash_loop(first, q_ref, k_ref, v_ref, vaug_ref, s_ref, m_ref, acc_ref, tk):
    n = k_ref.shape[0] // tk
    assert n <= FLASH_MAX_TILES
    nc = tk // LANES
    dv = v_ref.shape[1]

    @pl.when(first)
    def _():
        lane = lax.broadcasted_iota(jnp.int32, (v_ref.shape[0], LANES), 1)
        vaug_ref[:, 0:dv] = v_ref[...]
        vaug_ref[:, dv:] = jnp.where(lane == 0, 1.0, 0.0).astype(BF16)

    def scores(j):
        s_ref[j % 2] = lax.dot_general(q_ref[...], k_ref[j * tk:(j + 1) * tk, :],
                                       (((1,), (1,)), ((), ())), preferred_element_type=F32)

    def softmax_pv(j):
        slot = j % 2
        cmax = s_ref[slot, :, 0:LANES]
        for c in range(1, nc):
            cmax = jnp.maximum(cmax, s_ref[slot, :, c * LANES:(c + 1) * LANES])
        m_prev = m_ref[...]
        m_new = jnp.maximum(m_prev, jnp.max(cmax, axis=1, keepdims=True))
        a = jnp.exp2(m_prev - m_new)
        p = jnp.concatenate(
            [jnp.exp2(s_ref[slot, :, c * LANES:(c + 1) * LANES] - m_new).astype(BF16)
             for c in range(nc)], axis=1)
        pv = jnp.dot(p, vaug_ref[j * tk:(j + 1) * tk, :], preferred_element_type=F32)
        acc_ref[...] = jnp.concatenate([a, a], axis=1) * acc_ref[...] + pv
        m_ref[...] = m_new

    m_ref[...] = jnp.full(m_ref.shape, -jnp.inf, F32)
    acc_ref[...] = jnp.zeros(acc_ref.shape, F32)
    scores(0)
    for j in range(n):
        if j + 1 < n:
            scores(j + 1)
        softmax_pv(j)
    return acc_ref[:, 0:dv], acc_ref[:, dv:dv + 1]


def _segment_call(kernel_fn, *, grid, in_specs, args, out_spec, out_cols, rows, scratch, name):
    return pl.pallas_call(
        kernel_fn,
        grid=grid,
        in_specs=in_specs,
        out_specs=out_spec,
        out_shape=jax.ShapeDtypeStruct((rows, out_cols), BF16),
        scratch_shapes=scratch,
        compiler_params=_cparams(("arbitrary",) * len(grid)),
        name=name,
    )(*args)


def _flash_tk(seq):
    return seq // 2 if seq // 2 <= 2 * FLASH_TK else FLASH_TK


def _diff_attn_kernel(lam_ref, g_ref, q_ref, k_ref, v_ref, o_ref, q_scr, *scr, tk, lambda_init):
    tq = q_ref.shape[0]
    q = q_ref[...]
    lane = lax.broadcasted_iota(jnp.int32, q.shape, 1)
    zero = jnp.zeros_like(q)
    q_scr[0:tq] = jnp.where(lane < DIFF_QK_DIM, q, zero)
    q_scr[tq:2 * tq] = jnp.where(lane >= DIFF_QK_DIM, q, zero)
    acc, l = _flash_loop(pl.program_id(2) == 0, q_scr, k_ref, v_ref, *scr, tk)
    o = acc / l
    lv = lam_ref[...].astype(F32)
    lam = (jnp.exp(jnp.sum(lv[0:1] * lv[1:2], axis=1, keepdims=True))
           - jnp.exp(jnp.sum(lv[2:3] * lv[3:4], axis=1, keepdims=True)) + lambda_init)
    o = o[:tq] - lam * o[tq:]
    y = o * lax.rsqrt(jnp.mean(o * o, axis=1, keepdims=True) + SUBLN_EPS)
    o_ref[...] = (y * g_ref[...] * (1.0 - lambda_init)).astype(o_ref.dtype)


def _diff_attention(h, lam_vec, subln_g, row0, batch, seq, lambda_init):
    tq = FLASH_ROWS // 2
    tk = _flash_tk(seq)
    assert seq % tq == 0 and seq % tk == 0 and row0 % seq == 0
    nq = seq // tq
    hcols = DIFF_HEADS
    qrow = lambda b, hh, i: (row0 // tq + b * nq + i, hh)
    return _segment_call(
        functools.partial(_diff_attn_kernel, tk=tk, lambda_init=lambda_init),
        grid=(batch, DIFF_HEADS, nq),
        in_specs=[
            pl.BlockSpec((4, DIFF_QK_DIM), lambda b, hh, i: (0, 0)),
            pl.BlockSpec((1, HEAD_DIM), lambda b, hh, i: (0, 0)),
            pl.BlockSpec((tq, HEAD_DIM), qrow),
            pl.BlockSpec((seq, HEAD_DIM), lambda b, hh, i: (row0 // seq + b, hcols + hh)),
            pl.BlockSpec((seq, HEAD_DIM), lambda b, hh, i: (row0 // seq + b, 2 * hcols + hh)),
        ],
        args=[lam_vec, subln_g.reshape(1, HEAD_DIM), h, h, h],
        out_spec=pl.BlockSpec((tq, HEAD_DIM), lambda b, hh, i: (b * nq + i, hh)),
        out_cols=DIFF_HEADS * HEAD_DIM, rows=batch * seq,
        scratch=_flash_scratch(2 * tq, HEAD_DIM, HEAD_DIM, tk, seq), name="diff_attn")


def _mla_attn_kernel(q_ref, k_ref, v_ref, o_ref, q_scr, *scr, tk):
    del q_scr
    acc, l = _flash_loop(pl.program_id(2) == 0, q_ref, k_ref, v_ref, *scr, tk)
    o_ref[...] = (acc / l).astype(o_ref.dtype)


def _mla_attention(qm, km, vm, row0, batch, seq):
    tq = FLASH_ROWS
    tk = _flash_tk(seq)
    assert seq % tq == 0 and seq % tk == 0 and row0 % seq == 0
    nq = seq // tq
    dqk = 2 * LANES
    qrow = lambda b, hh, i: (row0 // tq + b * nq + i, hh)
    return _segment_call(
        functools.partial(_mla_attn_kernel, tk=tk),
        grid=(batch, MLA_HEADS, nq),
        in_specs=[
            pl.BlockSpec((tq, dqk), qrow),
            pl.BlockSpec((seq, dqk), lambda b, hh, i: (row0 // seq + b, hh)),
            pl.BlockSpec((seq, MLA_V), lambda b, hh, i: (row0 // seq + b, hh)),
        ],
        args=[qm, km, vm],
        out_spec=pl.BlockSpec((tq, MLA_V), lambda b, hh, i: (b * nq + i, hh)),
        out_cols=MLA_HEADS * MLA_V, rows=batch * seq,
        scratch=_flash_scratch(tq, 8, MLA_V, tk, seq), name="mla_attn")


def _gqa_attn_kernel(q_ref, k_ref, v_ref, o_ref, q_scr, *scr, tk, group):
    tq = q_ref.shape[0]
    for g in range(group):
        q_scr[g * tq:(g + 1) * tq] = q_ref[:, g * HEAD_DIM:(g + 1) * HEAD_DIM]
    acc, l = _flash_loop(pl.program_id(2) == 0, q_scr, k_ref, v_ref, *scr, tk)
    o = acc / l
    for g in range(group):
        o_ref[:, g * HEAD_DIM:(g + 1) * HEAD_DIM] = o[g * tq:(g + 1) * tq].astype(o_ref.dtype)


def _gqa_attention(qg, kg, h, v_col0, row0, batch, seq):
    group = GQA_HEADS // GQA_KV_HEADS
    tq = FLASH_ROWS // group
    tk = _flash_tk(seq)
    assert seq % tq == 0 and seq % tk == 0 and row0 % seq == 0
    nq = seq // tq
    qrow = lambda b, n, i: (row0 // tq + b * nq + i, n)
    return _segment_call(
        functools.partial(_gqa_attn_kernel, tk=tk, group=group),
        grid=(batch, GQA_KV_HEADS, nq),
        in_specs=[
            pl.BlockSpec((tq, group * HEAD_DIM), qrow),
            pl.BlockSpec((seq, HEAD_DIM), lambda b, n, i: (row0 // seq + b, n)),
            pl.BlockSpec((seq, HEAD_DIM), lambda b, n, i: (row0 // seq + b, v_col0 + n)),
        ],
        args=[qg, kg, h],
        out_spec=pl.BlockSpec((tq, group * HEAD_DIM), lambda b, n, i: (b * nq + i, n)),
        out_cols=GQA_HEADS * HEAD_DIM, rows=batch * seq,
        scratch=_flash_scratch(group * tq, HEAD_DIM, HEAD_DIM, tk, seq), name="gqa_attn")


def _na_bias_table(rpb):
    cols = jnp.arange(GRID_W)
    start = jnp.clip(cols - NA_KW // 2, 0, GRID_W - NA_KW)
    kc = cols[None, :]
    valid = (kc >= start[:, None]) & (kc < start[:, None] + NA_KW)
    dc = jnp.clip(kc - cols[:, None] + (NA_KW - 1), 0, 2 * NA_KW - 2)
    tab = rpb.astype(F32)[:, :, dc]
    tab = jnp.where(valid[None, None], tab, NEG_BIG)
    return jnp.concatenate([tab[:, :-1], tab[:, 1:]], axis=-1) * LOG2E


def _na_kernel(q_ref, k0_ref, k1_ref, k2_ref, k3_ref, v0_ref, v1_ref, v2_ref, v3_ref, bias_ref, o_ref,
               *, rows, scale):
    i = pl.program_id(1)
    k_refs = (k0_ref, k1_ref, k2_ref, k3_ref)
    v_refs = (v0_ref, v1_ref, v2_ref, v3_ref)
    key_row0 = jnp.clip(i * NA_ROWS - NA_KH // 2, 0, rows - 2 * NA_ROWS)
    d = i * NA_ROWS - key_row0
    n_pairs = NA_ROWS

    lane = lax.broadcasted_iota(jnp.int32, (8, LANES), 1)
    row_mask = []
    for j in range(NA_ROWS):
        first = jnp.clip(j + d - NA_KH // 2, 0, 2 * NA_ROWS - NA_KH)
        per_pair = []
        for m in range(n_pairs):
            key_row = 2 * m + jnp.where(lane >= GRID_W, 1, 0)
            madd = jnp.where(key_row >= first, jnp.where(key_row < first + NA_KH, 0.0, NEG_BIG), NEG_BIG)
            per_pair.append(jnp.concatenate([madd] * (GRID_W // 8), axis=0))
        row_mask.append(per_pair)

    for hh in range(NA_HEADS):
        cs = slice(hh * HEAD_DIM, (hh + 1) * HEAD_DIM)
        q = (q_ref[:, cs].astype(F32) * scale).astype(BF16)
        s = [lax.dot_general(q, kr[:, cs], (((1,), (1,)), ((), ())), preferred_element_type=F32)
             for kr in k_refs]
        cols = []
        for m in range(n_pairs):
            c, half = divmod(m, n_pairs // 4)
            pieces = []
            for j in range(NA_ROWS):
                dr = jnp.clip(2 * m - j - d + (NA_KH - 1), 0, 2 * NA_KH - 3)
                sb = s[c][j * GRID_W:(j + 1) * GRID_W, half * LANES:(half + 1) * LANES]
                pieces.append(sb + bias_ref[hh, dr] + row_mask[j][m])
            cols.append(jnp.concatenate(pieces, axis=0))
        cmax = cols[0]
        for col in cols[1:]:
            cmax = jnp.maximum(cmax, col)
        mx = jnp.max(cmax, axis=1, keepdims=True)
        ps = [jnp.exp2(col - mx) for col in cols]
        psum = ps[0]
        for p in ps[1:]:
            psum = psum + p
        l = jnp.sum(psum, axis=1, keepdims=True)
        per_q = n_pairs // 4
        o = None
        for c, vr in enumerate(v_refs):
            pc = jnp.concatenate([p.astype(BF16) for p in ps[c * per_q:(c + 1) * per_q]], axis=1)
            oc = jnp.dot(pc, vr[:, cs], preferred_element_type=F32)
            o = oc if o is None else o + oc
        o_ref[:, cs] = (o / l).astype(o_ref.dtype)


def _na_attention(h, bias_tab, col0, row0, batch, seq):
    rows = seq // GRID_W
    assert NA_ROWS == NA_KH and NA_ROWS % 4 == 0
    assert seq % GRID_W == 0 and rows % NA_ROWS == 0 and rows >= 2 * NA_ROWS
    nb = rows // NA_ROWS
    blk = NA_ROWS * GRID_W
    qtr = blk // 2
    width = NA_HEADS * HEAD_DIM
    assert row0 % blk == 0
    qbase = row0 // blk
    kbase = row0 // qtr
    per_seq = seq // qtr

    def win(c, col):
        def index(b, i):
            first = jnp.clip(2 * i - 1, 0, per_seq - 4)
            return (kbase + b * per_seq + first + c, col)
        return pl.BlockSpec((qtr, width), index)

    qspec = lambda col: pl.BlockSpec((blk, width), lambda b, i: (qbase + b * nb + i, col))
    return _segment_call(
        functools.partial(_na_kernel, rows=rows, scale=HEAD_DIM ** -0.5 * LOG2E),
        grid=(batch, nb),
        in_specs=[qspec(col0)] + [win(c, col0 + 1) for c in range(4)] + [win(c, col0 + 2) for c in range(4)]
                 + [pl.BlockSpec(bias_tab.shape, lambda b, i: (0, 0, 0, 0))],
        args=[h] * 9 + [bias_tab],
        out_spec=pl.BlockSpec((blk, width), lambda b, i: (b * nb + i, 0)),
        out_cols=width, rows=batch * seq,
        scratch=[], name="na_attn")


def _rms(x, g, eps):
    return x * lax.rsqrt(jnp.mean(x * x, axis=1, keepdims=True) + eps) * g


def _odd_prep_kernel(h_ref, wq_ref, wk_ref, wv_ref, qn_ref, kvn_ref, gqn_ref, gkn_ref,
                     mtab_ref, atab_ref, qm_ref, km_ref, vm_ref, qg_ref, kg_ref, *, cols):
    c_qc, c_kvc, c_gq, c_gk, c_kpe = cols
    mla_scale = (MLA_NOPE + MLA_ROPE) ** -0.5 * LOG2E
    gqa_scale = HEAD_DIM ** -0.5 * LOG2E
    mc, ms1, ms2 = mtab_ref[0], mtab_ref[1], mtab_ref[2]
    ac, as1, as2 = atab_ref[0], atab_ref[1], atab_ref[2]
    mhalf = MLA_ROPE // 2
    ahalf = HEAD_DIM // 4

    qc = _rms(h_ref[:, c_qc:c_qc + MLA_Q_RANK].astype(F32), qn_ref[...], RMS_EPS)
    q = jnp.dot(qc.astype(BF16), wq_ref[...], preferred_element_type=F32)
    for hh in range(MLA_HEADS):
        lo = hh * 2 * LANES
        qm_ref[:, lo:lo + LANES] = (q[:, lo:lo + LANES] * mla_scale).astype(qm_ref.dtype)
        qr = _apply_rope(q[:, lo + LANES:lo + 2 * LANES], mc, ms1, ms2, mhalf)
        qm_ref[:, lo + LANES:lo + 2 * LANES] = (qr * mla_scale).astype(qm_ref.dtype)

    kvc = _rms(h_ref[:, c_kvc:c_kvc + MLA_KV_RANK].astype(F32), kvn_ref[...], RMS_EPS).astype(BF16)
    kn = jnp.dot(kvc, wk_ref[...], preferred_element_type=F32)
    vm_ref[...] = jnp.dot(kvc, wv_ref[...], preferred_element_type=F32).astype(vm_ref.dtype)
    kpe = _apply_rope(h_ref[:, c_kpe:c_kpe + LANES].astype(F32), mc, ms1, ms2, mhalf)
    kpe = kpe.astype(km_ref.dtype)
    for hh in range(MLA_HEADS):
        lo = hh * 2 * LANES
        km_ref[:, lo:lo + LANES] = kn[:, hh * LANES:(hh + 1) * LANES].astype(km_ref.dtype)
        km_ref[:, lo + LANES:lo + 2 * LANES] = kpe

    for hh in range(GQA_HEADS):
        sl = slice(c_gq + hh * HEAD_DIM, c_gq + (hh + 1) * HEAD_DIM)
        x = _rms(h_ref[:, sl].astype(F32), gqn_ref[...], RMS_EPS)
        x = _apply_rope(x, ac, as1, as2, ahalf) * gqa_scale
        qg_ref[:, hh * HEAD_DIM:(hh + 1) * HEAD_DIM] = x.astype(qg_ref.dtype)
    for hh in range(GQA_KV_HEADS):
        sl = slice(c_gk + hh * HEAD_DIM, c_gk + (hh + 1) * HEAD_DIM)
        x = _rms(h_ref[:, sl].astype(F32), gkn_ref[...], RMS_EPS)
        x = _apply_rope(x, ac, as1, as2, ahalf)
        kg_ref[:, hh * HEAD_DIM:(hh + 1) * HEAD_DIM] = x.astype(kg_ref.dtype)


def _odd_prep(h, wq, wk, wv, qn, kvn, gqn, gkn, mtab, atab, cols):
    t, n = h.shape
    tm = ROW_TM
    assert t % tm == 0
    row = lambda i: (i, 0)
    full = lambda i: (0, 0)
    tab = lambda i: (0, i, 0)
    widths = (MLA_HEADS * 2 * LANES, MLA_HEADS * 2 * LANES, MLA_HEADS * MLA_V,
              GQA_HEADS * HEAD_DIM, GQA_KV_HEADS * HEAD_DIM)
    return pl.pallas_call(
        functools.partial(_odd_prep_kernel, cols=cols),
        grid=(t // tm,),
        in_specs=[pl.BlockSpec((tm, n), row),
                  pl.BlockSpec(wq.shape, full), pl.BlockSpec(wk.shape, full), pl.BlockSpec(wv.shape, full),
                  pl.BlockSpec(qn.shape, full), pl.BlockSpec(kvn.shape, full),
                  pl.BlockSpec(gqn.shape, full), pl.BlockSpec(gkn.shape, full),
                  pl.BlockSpec((3, tm, LANES), tab), pl.BlockSpec((3, tm, LANES), tab)],
        out_specs=[pl.BlockSpec((tm, w), row) for w in widths],
        out_shape=[jax.ShapeDtypeStruct((t, w), BF16) for w in widths],
        compiler_params=_cparams(("arbitrary",)),
        name="odd_prep",
    )(h, wq, wk, wv, qn, kvn, gqn, gkn, mtab, atab)


def _layer_norm(y, g, b):
    mu = jnp.mean(y, axis=1, keepdims=True)
    yc = y - mu
    var = jnp.mean(yc * yc, axis=1, keepdims=True)
    return yc * lax.rsqrt(var + LN_EPS) * g + b


def _route(logits):
    lane = lax.broadcasted_iota(jnp.int32, logits.shape, 1)
    lanef = lane.astype(F32)
    big = float(LANES)
    gmask = lane < N_GROUPS
    gl = jnp.where(gmask, logits, -jnp.inf)
    gmax = jnp.max(gl, axis=1, keepdims=True)
    gsel = jnp.min(jnp.where(gl == gmax, lanef, big), axis=1, keepdims=True)
    gden = jnp.sum(jnp.where(gmask, jnp.exp(gl - gmax), 0.0), axis=1, keepdims=True)
    gprob = 1.0 / gden
    lo = N_GROUPS + EXPERTS_PER_GROUP * gsel
    emask = (lanef >= lo) & (lanef < lo + EXPERTS_PER_GROUP)
    el = jnp.where(emask, logits, -jnp.inf)
    v1 = jnp.max(el, axis=1, keepdims=True)
    i1 = jnp.min(jnp.where(el == v1, lanef, big), axis=1, keepdims=True)
    el2 = jnp.where(lanef == i1, -jnp.inf, el)
    v2 = jnp.max(el2, axis=1, keepdims=True)
    i2 = jnp.min(jnp.where(el2 == v2, lanef, big), axis=1, keepdims=True)
    e = jnp.exp(v2 - v1)
    g1 = gprob / (1.0 + e)
    g2 = gprob * e / (1.0 + e)
    gate = jnp.where(lane == 0, g1, jnp.where(lane == 1, g2, 0.0))
    return i1 - N_GROUPS, i2 - N_GROUPS, gate


def _rank_in_expert(e1, e2, cnt_ref):
    tm = e1.shape[0]
    lanef = lax.broadcasted_iota(jnp.int32, (tm, LANES), 1).astype(F32)
    sel1 = lanef == e1
    sel2 = lanef == e2
    both = jnp.where(sel1, 1.0, jnp.where(sel2, 1.0, 0.0))
    row = lax.broadcasted_iota(jnp.int32, (tm, tm), 0)
    col = lax.broadcasted_iota(jnp.int32, (tm, tm), 1)
    ltri = jnp.where(col < row, 1.0, 0.0).astype(BF16)
    before = jnp.dot(ltri, both.astype(BF16), preferred_element_type=F32) + cnt_ref[...]
    r1 = jnp.sum(jnp.where(sel1, before, 0.0), axis=1, keepdims=True)
    r2 = jnp.sum(jnp.where(sel2, before, 0.0), axis=1, keepdims=True)
    cnt_ref[...] = cnt_ref[...] + jnp.sum(both, axis=0, keepdims=True)
    return r1, r2


def _pick_part(refs, ends, step):
    x = refs[-1][...]
    for ref, end in zip(refs[-2::-1], ends[-2::-1]):
        x = jnp.where(step < end, ref[...], x)
    return x


def _out_ln_route_kernel(*refs, alpha, ends):
    n1, n2, n3 = (len(e) for e in ends)
    a1_refs, a2_refs, x_refs = refs[:n1], refs[n1:n1 + n2], refs[n1 + n2:n1 + n2 + n3]
    w_ref, g_ref, b_ref, wr_ref, br_ref, x1_ref, meta_ref, gate_ref, cnt_ref = refs[n1 + n2 + n3:]
    step = pl.program_id(0)

    @pl.when(step == 0)
    def _():
        cnt_ref[...] = jnp.zeros(cnt_ref.shape, F32)

    a1 = _pick_part(a1_refs, ends[0], step)
    a2 = _pick_part(a2_refs, ends[1], step)
    x = _pick_part(x_refs, ends[2], step)
    k1 = a1.shape[1]
    mix = (jnp.dot(a1, w_ref[0:k1, :], preferred_element_type=F32)
           + jnp.dot(a2, w_ref[k1:, :], preferred_element_type=F32))
    x1 = _layer_norm(alpha * x + mix, g_ref[...], b_ref[...])
    x1_ref[...] = x1
    x_hi = x1.astype(BF16)
    x_mid = (x1 - x_hi.astype(F32)).astype(BF16)
    t = (jnp.dot(x_hi, wr_ref[...], preferred_element_type=F32)
         + jnp.dot(x_mid, wr_ref[...], preferred_element_type=F32))
    logits = t[:, :LANES] + t[:, LANES:] + br_ref[...]
    e1, e2, gate = _route(logits)
    r1, r2 = _rank_in_expert(e1, e2, cnt_ref)
    lane = lax.broadcasted_iota(jnp.int32, gate.shape, 1)
    meta = jnp.where(lane == 0, e1, jnp.where(lane == 1, e2,
                                              jnp.where(lane == 2, r1, jnp.where(lane == 3, r2, 0.0))))
    meta_ref[...] = meta.astype(jnp.int32)
    gate_ref[...] = gate


def _part_specs(parts, tm):
    specs, ends, start = [], [], 0
    for p in parts:
        assert p.shape[0] % tm == 0
        n = p.shape[0] // tm
        specs.append(pl.BlockSpec((tm, p.shape[1]), functools.partial(
            lambda i, start, n: (jnp.clip(i - start, 0, n - 1), 0), start=start, n=n)))
        start += n
        ends.append(start)
    return specs, tuple(ends)


def _out_ln_route(a1_parts, a2_parts, x_parts, w, g, b, wr, br, alpha):
    d = x_parts[0].shape[1]
    t = sum(xp.shape[0] for xp in x_parts)
    tm = ROW_TM
    row = lambda i: (i, 0)
    full = lambda i: (0, 0)
    spec_lists, ends = zip(*(_part_specs(parts, tm) for parts in (a1_parts, a2_parts, x_parts)))
    assert all(e[-1] == t // tm for e in ends)
    return pl.pallas_call(
        functools.partial(_out_ln_route_kernel, alpha=alpha, ends=ends),
        grid=(t // tm,),
        in_specs=[*spec_lists[0], *spec_lists[1], *spec_lists[2],
                  pl.BlockSpec(w.shape, full),
                  pl.BlockSpec((1, d), full), pl.BlockSpec((1, d), full),
                  pl.BlockSpec(wr.shape, full), pl.BlockSpec((1, LANES), full)],
        out_specs=[pl.BlockSpec((tm, d), row), pl.BlockSpec((tm, LANES), row),
                   pl.BlockSpec((tm, LANES), row), pl.BlockSpec((1, LANES), full)],
        out_shape=[jax.ShapeDtypeStruct((t, d), F32), jax.ShapeDtypeStruct((t, LANES), jnp.int32),
                   jax.ShapeDtypeStruct((t, LANES), F32), jax.ShapeDtypeStruct((1, LANES), F32)],
        compiler_params=_cparams(("arbitrary",)),
        name="out_ln_route",
    )(*a1_parts, *a2_parts, *x_parts, w, g, b, wr, br)


def _row_copy(src_ref, src_row, dst_ref, dst_row, sem):
    return pltpu.make_async_copy(src_ref.at[pl.ds(src_row, 1)], dst_ref.at[pl.ds(dst_row, 1)], sem)


def _dispatch_kernel(dest_ref, x_ref, xs_in_ref, xs_ref, sem):
    del xs_in_ref
    tm = x_ref.shape[0]

    def start(r, _):
        for k in range(TOP_K):
            _row_copy(x_ref, r, xs_ref, dest_ref[0, 0, TOP_K * r + k], sem).start()
        return 0

    def wait(r, _):
        for k in range(TOP_K):
            _row_copy(x_ref, r, xs_ref, dest_ref[0, 0, TOP_K * r + k], sem).wait()
        return 0

    lax.fori_loop(0, tm, start, 0, unroll=DMA_UNROLL)
    lax.fori_loop(0, tm, wait, 0, unroll=DMA_UNROLL)


def _dispatch(dest2d, x, xs_init):
    t, d = x.shape
    tm = ROW_TM
    return pl.pallas_call(
        _dispatch_kernel,
        grid=(t // tm,),
        in_specs=[pl.BlockSpec((1, 1, TOP_K * tm), lambda i: (i, 0, 0), memory_space=pltpu.SMEM),
                  pl.BlockSpec((tm, d), lambda i: (i, 0)),
                  pl.BlockSpec(memory_space=pl.ANY)],
        out_specs=pl.BlockSpec(memory_space=pl.ANY),
        out_shape=jax.ShapeDtypeStruct(xs_init.shape, xs_init.dtype),
        scratch_shapes=[pltpu.SemaphoreType.DMA(())],
        input_output_aliases={2: 0},
        compiler_params=_cparams(("arbitrary",)),
        name="moe_dispatch",
    )(dest2d, x, xs_init)


def _expert_kernel(be_ref, nv_ref, xs_ref, w1_ref, w3_ref, w2_ref, y_ref, w1b, w3b, w2b):
    b = pl.program_id(0)
    valid = b < nv_ref[0]
    new_expert = jnp.logical_or(b == 0, be_ref[b] != be_ref[jnp.maximum(b - 1, 0)])

    @pl.when(jnp.logical_and(valid, new_expert))
    def _():
        w1b[...] = w1_ref[0, 0].astype(BF16)
        w3b[...] = w3_ref[0, 0].astype(BF16)
        w2b[...] = w2_ref[0, 0].astype(BF16)

    @pl.when(valid)
    def _():
        x = xs_ref[...].astype(BF16)
        h1 = jnp.dot(x, w1b[...], preferred_element_type=F32)
        h3 = jnp.dot(x, w3b[...], preferred_element_type=F32)
        hm = (h1 / (1.0 + jnp.exp(-h1)) * h3).astype(BF16)
        y_ref[...] = jnp.dot(hm, w2b[...], preferred_element_type=F32).astype(y_ref.dtype)

    @pl.when(jnp.logical_not(valid))
    def _():
        y_ref[...] = jnp.zeros_like(y_ref)


def _experts(block_e, n_valid, xs, w1, w3, w2, layer):
    p, d = xs.shape
    tb = MOE_TB
    nb = p // tb
    de = w1.shape[3]
    last = lambda b, nv: jnp.minimum(b, nv[0] - 1)
    wmap = lambda b, be, nv: (layer, be[last(b, nv)], 0, 0)
    return pl.pallas_call(
        _expert_kernel,
        grid_spec=pltpu.PrefetchScalarGridSpec(
            num_scalar_prefetch=2,
            grid=(nb,),
            in_specs=[pl.BlockSpec((tb, d), lambda b, be, nv: (last(b, nv), 0)),
                      pl.BlockSpec((1, 1, d, de), wmap), pl.BlockSpec((1, 1, d, de), wmap),
                      pl.BlockSpec((1, 1, de, d), wmap)],
            out_specs=pl.BlockSpec((tb, d), lambda b, be, nv: (b, 0)),
            scratch_shapes=[pltpu.VMEM((d, de), BF16), pltpu.VMEM((d, de), BF16),
                            pltpu.VMEM((de, d), BF16)],
        ),
        out_shape=jax.ShapeDtypeStruct((p, d), F32),
        compiler_params=_cparams(("arbitrary",)),
        name="moe_experts",
    )(block_e, n_valid, xs, w1, w3, w2)


def _combine_ln_kernel(dest_ref, next_dest_ref, gate_ref, x_ref, g_ref, b_ref, y_ref, o_ref, *rest, alpha):
    *maybe_ob_ref, buf, sems = rest
    tm = x_ref.shape[0]
    step = pl.program_id(0)
    last = pl.num_programs(0) - 1
    slot = step % 2
    other = 1 - slot

    def gather(idx_ref, r, into):
        return [_row_copy(y_ref, idx_ref[0, 0, TOP_K * r + k], buf.at[into, k], r, sems.at[into])
                for k in range(TOP_K)]

    def start_rows(idx_ref, into, lo, n):
        def body(r, _):
            for cp in gather(idx_ref, lo + r, into):
                cp.start()
            return 0
        lax.fori_loop(0, n, body, 0, unroll=DMA_UNROLL)

    def wait_rows(idx_ref, into):
        def body(r, _):
            for cp in gather(idx_ref, r, into):
                cp.wait()
            return 0
        lax.fori_loop(0, tm, body, 0, unroll=DMA_UNROLL)

    @pl.when(step == 0)
    def _():
        start_rows(dest_ref, slot, 0, tm)

    wait_rows(dest_ref, slot)

    start_rows(next_dest_ref, other, 0, tm)
    gate = gate_ref[...]
    ff = buf[slot, 0] * gate[:, 0:1] + buf[slot, 1] * gate[:, 1:2]
    x2 = _layer_norm(alpha * x_ref[...] + ff, g_ref[...], b_ref[...])
    o_ref[...] = x2
    for ob_ref in maybe_ob_ref:
        ob_ref[...] = x2.astype(ob_ref.dtype)

    @pl.when(step == last)
    def _():
        wait_rows(next_dest_ref, other)


def _combine_ln(dest2d, gate, x, g, b, y, alpha, row0, nrows, with_bf16):
    d = x.shape[1]
    tm = ROW_TM
    assert row0 % tm == 0 and nrows % tm == 0
    base = row0 // tm
    row_in = lambda i: (base + i, 0)
    row_out = lambda i: (i, 0)
    full = lambda i: (0, 0)
    n_out = 2 if with_bf16 else 1
    n_steps = nrows // tm
    idx_spec = lambda f: pl.BlockSpec((1, 1, TOP_K * tm), f, memory_space=pltpu.SMEM)
    return pl.pallas_call(
        functools.partial(_combine_ln_kernel, alpha=alpha),
        grid=(n_steps,),
        in_specs=[idx_spec(lambda i: (base + i, 0, 0)),
                  idx_spec(lambda i: (base + jnp.minimum(i + 1, n_steps - 1), 0, 0)),
                  pl.BlockSpec((tm, LANES), row_in), pl.BlockSpec((tm, d), row_in),
                  pl.BlockSpec((1, d), full), pl.BlockSpec((1, d), full),
                  pl.BlockSpec(memory_space=pl.ANY)],
        out_specs=[pl.BlockSpec((tm, d), row_out)] * n_out,
        out_shape=[jax.ShapeDtypeStruct((nrows, d), F32), jax.ShapeDtypeStruct((nrows, d), BF16)][:n_out],
        scratch_shapes=[pltpu.VMEM((2, TOP_K, tm, d), F32), pltpu.SemaphoreType.DMA((2,))],
        compiler_params=_cparams(("arbitrary",)),
        name="moe_combine_ln",
    )(dest2d, dest2d, gate, x, g, b, y)


def _moe_plan(eid, rank, counts, tb):
    a = eid.shape[0] * TOP_K
    padded = (counts + tb - 1) // tb * tb
    pend = jnp.cumsum(padded)
    pstart = pend - padded
    dest = (jnp.take(pstart, eid) + rank).astype(jnp.int32)
    nb = -(-a // tb) + N_EXPERTS
    block_row0 = jnp.arange(nb, dtype=jnp.int32) * tb
    block_e = jnp.minimum(jnp.sum((pend[None, :] <= block_row0[:, None]).astype(jnp.int32), axis=1),
                          N_EXPERTS - 1).astype(jnp.int32)
    n_valid = (pend[-1] // tb).astype(jnp.int32).reshape(1)
    return dest, block_e, n_valid, nb


def _segments(groups):
    segs, row0 = [], 0
    for b, s in groups:
        segs.append((row0, b, s))
        row0 += b * s
    return segs


def kernel(x_prompt, x_sample, ev_w_in, ev_w_out, diff_lambda, diff_subln, na_rpb, od_w_in, od_w_out,
           mla_q_norm, mla_w_q_up, mla_kv_norm, mla_w_kv_up, gqa_q_norm, gqa_k_norm,
           ln1_g, ln1_b, ln2_g, ln2_b, moe_w_group, moe_b_group, moe_w_expert, moe_b_expert,
           moe_w1, moe_w3, moe_w2):
    d = x_prompt.shape[-1]
    depth = ln1_g.shape[0]
    alpha = (2 * depth) ** 0.25
    groups = [(x_prompt.shape[0], x_prompt.shape[1]), (x_sample.shape[0], x_sample.shape[1])]
    segs = _segments(groups)
    x_parts = [x_prompt.reshape(-1, d), x_sample.reshape(-1, d)]
    t = sum(xp.shape[0] for xp in x_parts)
    xb = jnp.concatenate([xp.astype(BF16) for xp in x_parts], axis=0)
    pos = jnp.concatenate([jnp.tile(jnp.arange(s, dtype=jnp.int32), b) for b, s in groups])
    posf = pos.astype(F32)

    for l in range(depth):
        i = l // 2
        if l % 2 == 0:
            tab = _diff_tables(posf)
            qscale = DIFF_QK_DIM ** -0.5 * LOG2E
            tabs = jnp.stack([tab * qscale, tab])
            h = _even_proj(xb, ev_w_in[i].astype(BF16), tabs)
            lambda_init = 0.8 - 0.6 * math.exp(-0.3 * l)
            bias_tab = _na_bias_table(na_rpb[i])
            o_a = [_diff_attention(h, diff_lambda[i], diff_subln[i], r0, b, s, lambda_init)
                   for r0, b, s in segs]
            o_b = [_na_attention(h, bias_tab, 3, r0, b, s) for r0, b, s in segs]
            w_out = ev_w_out[i]
        else:
            s1 = MLA_Q_RANK
            s2 = s1 + MLA_KV_RANK
            s3 = s2 + MLA_ROPE
            w_in = od_w_in[i]
            w_perm = jnp.concatenate(
                [w_in[:, :s2], w_in[:, s3:], w_in[:, s2:s3], jnp.zeros((d, LANES - MLA_ROPE), w_in.dtype)],
                axis=1).astype(BF16)
            c_gq = s2
            c_gk = c_gq + GQA_HEADS * HEAD_DIM
            c_gv = c_gk + GQA_KV_HEADS * HEAD_DIM
            c_kpe = c_gv + GQA_KV_HEADS * HEAD_DIM
            h = _plain_proj(xb, w_perm)
            wq = mla_w_q_up[i].reshape(MLA_Q_RANK, MLA_HEADS, MLA_NOPE + MLA_ROPE)
            wq = jnp.pad(wq, ((0, 0), (0, 0), (0, 2 * LANES - MLA_NOPE - MLA_ROPE)))
            wq = wq.reshape(MLA_Q_RANK, MLA_HEADS * 2 * LANES).astype(BF16)
            wkv = mla_w_kv_up[i].reshape(MLA_KV_RANK, MLA_HEADS, MLA_NOPE + MLA_V)
            wk = wkv[:, :, :MLA_NOPE].reshape(MLA_KV_RANK, MLA_HEADS * MLA_NOPE).astype(BF16)
            wv = wkv[:, :, MLA_NOPE:].reshape(MLA_KV_RANK, MLA_HEADS * MLA_V).astype(BF16)
            qm, km, vm, qg, kg = _odd_prep(
                h, wq, wk, wv, mla_q_norm[i].reshape(1, -1), mla_kv_norm[i].reshape(1, -1),
                gqa_q_norm[i].reshape(1, -1), gqa_k_norm[i].reshape(1, -1),
                _mla_tables(posf), _axial_tables(pos), (0, s1, c_gq, c_gk, c_kpe))
            o_a = [_mla_attention(qm, km, vm, r0, b, s) for r0, b, s in segs]
            o_b = [_gqa_attention(qg, kg, h, c_gv // HEAD_DIM, r0, b, s) for r0, b, s in segs]
            w_out = od_w_out[i]

        wr = jnp.concatenate([moe_w_group[l], moe_w_expert[l]], axis=1)
        wr = jnp.pad(wr, ((0, 0), (0, LANES - wr.shape[1])))
        wr_hi = wr.astype(BF16)
        wr = jnp.concatenate([wr_hi, (wr - wr_hi.astype(F32)).astype(BF16)], axis=1)
        br = jnp.concatenate([moe_b_group[l], moe_b_expert[l]])
        br = jnp.pad(br, (0, LANES - br.shape[0])).reshape(1, LANES)
        x1, meta, gate, cnt = _out_ln_route(o_a, o_b, x_parts, w_out.astype(BF16), ln1_g[l].reshape(1, d),
                                            ln1_b[l].reshape(1, d), wr, br, alpha)

        dest, block_e, n_valid, nb = _moe_plan(meta[:, :TOP_K], meta[:, TOP_K:2 * TOP_K],
                                               cnt[0, :N_EXPERTS].astype(jnp.int32), MOE_TB)
        dest2d = dest.reshape(t // ROW_TM, 1, TOP_K * ROW_TM)
        xs = _dispatch(dest2d, x1, jnp.zeros((nb * MOE_TB, d), F32))
        y = _experts(block_e, n_valid, xs, moe_w1, moe_w3, moe_w2, l)
        ln2 = (ln2_g[l].reshape(1, d), ln2_b[l].reshape(1, d))
        if l + 1 < depth:
            x, xb = _combine_ln(dest2d, gate, x1, *ln2, y, alpha, 0, t, True)
            x_parts = [x]
        else:
            outs = [_combine_ln(dest2d, gate, x1, *ln2, y, alpha, r0, b * s, False)[0]
                    for r0, b, s in segs]

    return (outs[0].reshape(x_prompt.shape), outs[1].reshape(x_sample.shape))
```
